```python
import math
import jax, jax.numpy as jnp
from jax import lax
import numpy as np

D_MODEL = 1024
BATCH = 8
SEQ = 4096
DEPTH = 4
DEC_BATCH = 32
DEC_SEQ = 32
PAST_LEN = 4096

CHUNK = 64
N_MEM = 256
M_WIDTH = D_MODEL // 2
M_HEADS = 4
M_DIM = M_WIDTH // M_HEADS
R_WIDTH = D_MODEL - M_WIDTH
R_HEADS = 4
R_DIM = R_WIDTH // R_HEADS
CONV_W = 4
ROPE_BASE = 10000.0
IN_SPLITS = (2 * M_WIDTH, M_WIDTH, M_WIDTH, 2 * M_HEADS, R_WIDTH, R_WIDTH, R_WIDTH, R_WIDTH)
IN_WIDTH = sum(IN_SPLITS)
X_HEADS = 4
X_DIM = D_MODEL // X_HEADS
N_EXPERTS = 16
N_GROUPS = 4
EXP_PER_GROUP = N_EXPERTS // N_GROUPS
TOP_K = 2
D_EXPERT = D_MODEL // 2
ALPHA = (2 * DEPTH) ** 0.25
BETA = (8 * DEPTH) ** -0.25
LN_EPS = 1e-5

kernel_name = "hybrid_mlstm_retention_moe_stream_step"


def layer_norm(x, g, b):
    xf = x.astype(jnp.float32)
    mu = xf.mean(-1, keepdims=True)
    var = jnp.square(xf - mu).mean(-1, keepdims=True)
    y = (xf - mu) * lax.rsqrt(var + LN_EPS) * g.astype(jnp.float32) + b.astype(jnp.float32)
    return y.astype(x.dtype)


def head_norm(h, g):
    mu = h.mean(-1, keepdims=True)
    var = jnp.square(h - mu).mean(-1, keepdims=True)
    hn = (h - mu) * lax.rsqrt(var + LN_EPS)
    B, H, T, d = hn.shape
    return hn.transpose(0, 2, 1, 3).reshape(B, T, H * d) * g.astype(jnp.float32)


def split_cols(t, widths):
    out, o = [], 0
    for w in widths:
        out.append(t[..., o:o + w])
        o += w
    return out


def to_heads(t, n_heads):
    B, T, W = t.shape
    return t.reshape(B, T, n_heads, W // n_heads).transpose(0, 2, 1, 3)


def rotary(x, pos):
    half = x.shape[-1] // 2
    inv = ROPE_BASE ** (-jnp.arange(half, dtype=jnp.float32) / half)
    ang = pos.astype(jnp.float32)[:, None] * inv[None, :]
    cos, sin = jnp.cos(ang), jnp.sin(ang)
    x1, x2 = x[..., :half], x[..., half:]
    return jnp.concatenate([x1 * cos - x2 * sin, x1 * sin + x2 * cos], axis=-1)


def causal_conv(u, buf, w):
    T = u.shape[1]
    full = jnp.concatenate([buf.astype(u.dtype), u], axis=1)
    out = full[:, 0:T] * w[0]
    for j in range(1, CONV_W):
        out = out + full[:, j:j + T] * w[j]
    return out, full[:, -(CONV_W - 1):]


def mlstm_chunk(state, inp):
    C0, n0, m0 = state
    q, k, v, ig, lf = inp
    L = q.shape[2]
    b = jnp.cumsum(lf, axis=-1)
    a = b + m0[..., None]
    causal = jnp.tril(jnp.ones((L, L), dtype=bool))
    dmat = jnp.where(causal, b[..., :, None] - b[..., None, :] + ig[..., None, :], -jnp.inf)
    m = jnp.maximum(a, dmat.max(-1))
    w_intra = jnp.exp(dmat - m[..., None])
    w_inter = jnp.exp(a - m)
    s = jnp.einsum('bhtd,bhsd->bhts', q, k) * w_intra
    num = jnp.einsum('bhts,bhse->bhte', s, v) + w_inter[..., None] * jnp.einsum('bhtd,bhde->bhte', q, C0)
    den = s.sum(-1) + w_inter * jnp.einsum('bhtd,bhd->bht', q, n0)
    h = num / jnp.maximum(jnp.abs(den), jnp.exp(-m))[..., None]
    m_last = m[..., -1]
    wk = jnp.exp(b[..., -1:] - b + ig - m_last[..., None])
    decay = jnp.exp(a[..., -1] - m_last)
    C1 = decay[..., None, None] * C0 + jnp.einsum('bhs,bhsd,bhse->bhde', wk, k, v)
    n1 = decay[..., None] * n0 + jnp.einsum('bhs,bhsd->bhd', wk, k)
    return (C1, n1, m_last), h


def retention_chunk(S0, inp):
    q, k, v = inp
    L = q.shape[2]
    log_g = jnp.log(1.0 - 2.0 ** (-5.0 - jnp.arange(R_HEADS, dtype=jnp.float32)))
    idx = jnp.arange(L, dtype=jnp.float32)
    rel = idx[:, None] - idx[None, :]
    dmask = jnp.where(rel >= 0, jnp.exp(log_g[:, None, None] * jnp.maximum(rel, 0.0)), 0.0)
    s = jnp.einsum('bhtd,bhsd->bhts', q, k) * dmask[None]
    inner = jnp.einsum('bhts,bhse->bhte', s, v)
    cross = jnp.einsum('bhtd,bhde->bhte', q, S0) * jnp.exp(log_g[:, None] * (idx + 1.0))[None, :, :, None]
    wk = jnp.exp(log_g[:, None] * (L - 1.0 - idx))
    S1 = jnp.exp(log_g * L)[None, :, None, None] * S0 + jnp.einsum('hs,bhsd,bhse->bhde', wk, k, v)
    return S1, inner + cross


def to_chunks(t):
    B, H, T = t.shape[:3]
    t = t.reshape(B, H, T // CHUNK, CHUNK, *t.shape[3:])
    return jnp.moveaxis(t, 2, 0)


def from_chunks(y):
    y = jnp.moveaxis(y, 0, 2)
    return y.reshape(y.shape[0], y.shape[1], -1, y.shape[-1])


def run_chunks(chunk_fn, state, inputs):
    T = inputs[0].shape[2]
    if T <= CHUNK:
        return chunk_fn(state, inputs)
    state, ys = lax.scan(chunk_fn, state, tuple(to_chunks(t) for t in inputs))
    return state, from_chunks(ys)


def parallel_mixer(x, pos, conv_buf, m_state, r_state, w_in, b_if, conv_w, gn_m, gn_r, w_out):
    B, T, _ = x.shape
    f32 = jnp.float32
    proj = x @ w_in
    qk_pre, mv, mo, gates, rq, rk, rv, rg = split_cols(proj, IN_SPLITS)
    qk, conv_new = causal_conv(qk_pre, conv_buf, conv_w)
    qk = jax.nn.silu(qk.astype(f32))
    mq = to_heads(qk[..., :M_WIDTH], M_HEADS) * (M_DIM ** -0.5)
    mk = to_heads(qk[..., M_WIDTH:], M_HEADS)
    mvh = to_heads(mv.astype(f32), M_HEADS)
    gates = gates.astype(f32) + b_if.astype(f32)
    ig = gates[..., :M_HEADS].transpose(0, 2, 1)
    lf = jax.nn.log_sigmoid(gates[..., M_HEADS:]).transpose(0, 2, 1)
    m_state = tuple(s.astype(f32) for s in m_state)
    m_state, h_m = run_chunks(mlstm_chunk, m_state, (mq, mk, mvh, ig, lf))
    h_m = head_norm(h_m, gn_m) * jax.nn.sigmoid(mo.astype(f32))
    rqh = rotary(to_heads(rq.astype(f32), R_HEADS), pos)
    rkh = rotary(to_heads(rk.astype(f32), R_HEADS), pos) * (R_DIM ** -0.5)
    rvh = to_heads(rv.astype(f32), R_HEADS)
    r_state, h_r = run_chunks(retention_chunk, r_state.astype(f32), (rqh, rkh, rvh))
    h_r = head_norm(h_r, gn_r) * jax.nn.silu(rg.astype(f32))
    out = jnp.concatenate([h_m, h_r], axis=-1).astype(x.dtype) @ w_out
    return out, conv_new, m_state, r_state


def mem_kv(mem, w_xk, w_xv):
    B, N, _ = mem.shape
    k = (mem @ w_xk).reshape(B, N, X_HEADS, X_DIM)
    v = (mem @ w_xv).reshape(B, N, X_HEADS, X_DIM)
    return k, v


def mem_attend(x, mem_k, mem_v, w_xq, w_xo):
    B, T, D = x.shape
    q = (x @ w_xq).reshape(B, T, X_HEADS, X_DIM)
    s = jnp.einsum('bthd,bnhd->bhtn', q, mem_k.astype(q.dtype)).astype(jnp.float32) * (X_DIM ** -0.5)
    p = jax.nn.softmax(s, axis=-1)
    o = jnp.einsum('bhtn,bnhd->bthd', p.astype(x.dtype), mem_v.astype(x.dtype)).reshape(B, T, D)
    return o @ w_xo


def moe_ffn(x, w_router, w_e_gate, w_e_up, w_e_down):
    B, T, D = x.shape
    xt = x.reshape(B * T, D)
    probs = jax.nn.softmax((xt @ w_router).astype(jnp.float32), axis=-1)
    grouped = probs.reshape(-1, N_GROUPS, EXP_PER_GROUP)
    group_score = lax.top_k(grouped, TOP_K)[0].sum(-1)
    group_sel = jnp.argmax(group_score, axis=-1)
    expert_group = jnp.arange(N_EXPERTS) // EXP_PER_GROUP
    masked = jnp.where(expert_group[None, :] == group_sel[:, None], probs, -1.0)
    top_p, top_i = lax.top_k(masked, TOP_K)
    top_w = top_p / top_p.sum(-1, keepdims=True)
    combine = jnp.einsum('nk,nke->ne', top_w, jax.nn.one_hot(top_i, N_EXPERTS, dtype=jnp.float32))
    y = jnp.zeros((B * T, D), jnp.float32)
    for e in range(N_EXPERTS):
        h = jax.nn.silu(xt @ w_e_gate[e]) * (xt @ w_e_up[e])
        y = y + combine[:, e:e + 1] * (h @ w_e_down[e]).astype(jnp.float32)
    return y.astype(x.dtype).reshape(B, T, D)


def block(x, pos, mem_k, mem_v, conv_buf, m_state, r_state,
          w_in, b_if, conv_w, gn_m, gn_r, w_out, w_xq, w_xo,
          w_router, w_e_gate, w_e_up, w_e_down, ln_g, ln_b):
    mix, conv_buf, m_state, r_state = parallel_mixer(x, pos, conv_buf, m_state, r_state,
                                                     w_in, b_if, conv_w, gn_m, gn_r, w_out)
    x = layer_norm(ALPHA * x + mix, ln_g[0], ln_b[0])
    x = layer_norm(ALPHA * x + mem_attend(x, mem_k, mem_v, w_xq, w_xo), ln_g[1], ln_b[1])
    x = layer_norm(ALPHA * x + moe_ffn(x, w_router, w_e_gate, w_e_up, w_e_down), ln_g[2], ln_b[2])
    return x, conv_buf, m_state, r_state


def setup_inputs(seed: int = 0) -> dict:
    key = jax.random.key(seed)
    ks = jax.random.split(key, 32)
    f32 = jnp.float32
    D = D_MODEL

    def nrm(k, shape, scale):
        return jax.random.normal(k, shape, f32) * scale

    b_if = jnp.concatenate([nrm(ks[10], (DEPTH, M_HEADS), 0.1),
                            jnp.linspace(3.0, 6.0, M_HEADS, dtype=f32)[None, :] + nrm(ks[11], (DEPTH, M_HEADS), 0.1)],
                           axis=-1)
    return {
        'x_prompt': nrm(ks[0], (BATCH, SEQ, D), 1.0),
        'x_sample': nrm(ks[1], (DEC_BATCH, DEC_SEQ, D), 1.0),
        'mem_prompt': nrm(ks[2], (BATCH, N_MEM, D), 1.0),
        'cache_mem_k': nrm(ks[3], (DEPTH, DEC_BATCH, N_MEM, X_HEADS, X_DIM), 1.0),
        'cache_mem_v': nrm(ks[4], (DEPTH, DEC_BATCH, N_MEM, X_HEADS, X_DIM), BETA),
        'state_conv': nrm(ks[5], (DEPTH, DEC_BATCH, CONV_W - 1, 2 * M_WIDTH), 1.0),
        'state_mlstm_C': nrm(ks[6], (DEPTH, DEC_BATCH, M_HEADS, M_DIM, M_DIM), 0.1),
        'state_mlstm_n': nrm(ks[7], (DEPTH, DEC_BATCH, M_HEADS, M_DIM), 0.1),
        'state_mlstm_m': nrm(ks[8], (DEPTH, DEC_BATCH, M_HEADS), 1.0),
        'state_ret_S': nrm(ks[9], (DEPTH, DEC_BATCH, R_HEADS, R_DIM, R_DIM), 0.1),
        'w_in': nrm(ks[12], (DEPTH, D, IN_WIDTH), D ** -0.5),
        'b_if': b_if,
        'conv_w': nrm(ks[13], (DEPTH, CONV_W, 2 * M_WIDTH), CONV_W ** -0.5),
        'gn_m': 1.0 + nrm(ks[14], (DEPTH, M_WIDTH), 0.02),
        'gn_r': 1.0 + nrm(ks[15], (DEPTH, R_WIDTH), 0.02),
        'w_out': nrm(ks[16], (DEPTH, D, D), BETA * D ** -0.5),
        'w_xq': nrm(ks[17], (DEPTH, D, D), D ** -0.5),
        'w_xk': nrm(ks[18], (DEPTH, D, D), D ** -0.5),
        'w_xv': nrm(ks[19], (DEPTH, D, D), BETA * D ** -0.5),
        'w_xo': nrm(ks[20], (DEPTH, D, D), BETA * D ** -0.5),
        'w_router': nrm(ks[21], (D, N_EXPERTS), D ** -0.5),
        'w_e_gate': nrm(ks[22], (DEPTH, N_EXPERTS, D, D_EXPERT), D ** -0.5),
        'w_e_up': nrm(ks[23], (DEPTH, N_EXPERTS, D, D_EXPERT), BETA * D ** -0.5),
        'w_e_down': nrm(ks[24], (DEPTH, N_EXPERTS, D_EXPERT, D), BETA * D_EXPERT ** -0.5),
        'ln_g': 1.0 + nrm(ks[25], (DEPTH, 3, D), 0.02),
        'ln_b': nrm(ks[26], (DEPTH, 3, D), 0.02),
    }


def reference(x_prompt, x_sample, mem_prompt, cache_mem_k, cache_mem_v, state_conv,
              state_mlstm_C, state_mlstm_n, state_mlstm_m, state_ret_S,
              w_in, b_if, conv_w, gn_m, gn_r, w_out, w_xq, w_xk, w_xv, w_xo,
              w_router, w_e_gate, w_e_up, w_e_down, ln_g, ln_b):
    f32 = jnp.float32
    Bp = x_prompt.shape[0]
    pos_p = jnp.arange(x_prompt.shape[1])
    pos_s = PAST_LEN + jnp.arange(x_sample.shape[1])
    xp, xs = x_prompt, x_sample
    mk_p, mv_p, cv_p, C_p, n_p, m_p, S_p = [], [], [], [], [], [], []
    cv_s, C_s, n_s, m_s, S_s = [], [], [], [], []
    for l in range(DEPTH):
        shared = (w_in[l], b_if[l], conv_w[l], gn_m[l], gn_r[l], w_out[l], w_xq[l], w_xo[l],
                  w_router, w_e_gate[l], w_e_up[l], w_e_down[l], ln_g[l], ln_b[l])
        kp, vp = mem_kv(mem_prompt, w_xk[l], w_xv[l])
        conv0 = jnp.zeros((Bp, CONV_W - 1, 2 * M_WIDTH), x_prompt.dtype)
        m0 = (jnp.zeros((Bp, M_HEADS, M_DIM, M_DIM), f32), jnp.zeros((Bp, M_HEADS, M_DIM), f32),
              jnp.zeros((Bp, M_HEADS), f32))
        r0 = jnp.zeros((Bp, R_HEADS, R_DIM, R_DIM), f32)
        xp, cb, ms, rs = block(xp, pos_p, kp, vp, conv0, m0, r0, *shared)
        mk_p.append(kp); mv_p.append(vp); cv_p.append(cb)
        C_p.append(ms[0]); n_p.append(ms[1]); m_p.append(ms[2]); S_p.append(rs)
        xs, cb, ms, rs = block(xs, pos_s, cache_mem_k[l], cache_mem_v[l], state_conv[l],
                               (state_mlstm_C[l], state_mlstm_n[l], state_mlstm_m[l]), state_ret_S[l], *shared)
        cv_s.append(cb); C_s.append(ms[0]); n_s.append(ms[1]); m_s.append(ms[2]); S_s.append(rs)
    return (xp, xs,
            jnp.stack(mk_p), jnp.stack(mv_p), jnp.stack(cv_p), jnp.stack(C_p), jnp.stack(n_p),
            jnp.stack(m_p), jnp.stack(S_p),
            jnp.stack(cv_s), jnp.stack(C_s), jnp.stack(n_s), jnp.stack(m_s), jnp.stack(S_s))
```

```python
import functools
import math

import numpy as np
import jax
import jax.numpy as jnp
from jax import lax
from jax.experimental import pallas as pl
from jax.experimental.pallas import tpu as pltpu

F32 = jnp.float32
_MXU_DTYPE = jnp.bfloat16

D_MODEL = 1024
PAST_LEN = 4096
N_MEM = 256
M_WIDTH = D_MODEL // 2
M_HEADS = 4
M_DIM = M_WIDTH // M_HEADS
R_WIDTH = D_MODEL - M_WIDTH
R_HEADS = 4
R_DIM = R_WIDTH // R_HEADS
CONV_W = 4
ROPE_BASE = 10000.0
X_HEADS = 4
X_DIM = D_MODEL // X_HEADS
N_EXPERTS = 16
N_GROUPS = 4
EXP_PER_GROUP = N_EXPERTS // N_GROUPS
D_EXPERT = D_MODEL // 2
LN_EPS = 1e-5
GATE_COL0 = 2 * M_WIDTH + 2 * M_WIDTH
N_GATES = 2 * M_HEADS
MAIN_WIDTH = 4 * D_MODEL

LANES = 128
SUBLANES = 8
VMEM_LIMIT_BYTES = 56 * 1024 * 1024

PROMPT_BLOCK_ROWS = 512
PROMPT_CHUNK = 256
SAMPLE_SEQS_PER_BLOCK = 8
MOE_BLOCK_ROWS = 1024


def _mx(a):
    return a.astype(_MXU_DTYPE)


def _dot(a, b):
    return jnp.dot(_mx(a), _mx(b), preferred_element_type=F32)


def _dot_nt(a, b):
    return lax.dot_general(_mx(a), _mx(b), (((1,), (1,)), ((), ())), preferred_element_type=F32)


def _dot_tn(a, b):
    return lax.dot_general(_mx(a), _mx(b), (((0,), (0,)), ((), ())), preferred_element_type=F32)


def _dot_exact(a, b):
    return jnp.dot(a, b, preferred_element_type=F32, precision=lax.Precision.HIGHEST)


def _layer_norm(y, g, b):
    mu = jnp.mean(y, axis=-1, keepdims=True)
    yc = y - mu
    var = jnp.mean(yc * yc, axis=-1, keepdims=True)
    return yc * lax.rsqrt(var + LN_EPS) * g + b


def _head_norm(h):
    mu = jnp.mean(h, axis=-1, keepdims=True)
    hc = h - mu
    var = jnp.mean(hc * hc, axis=-1, keepdims=True)
    return hc * lax.rsqrt(var + LN_EPS)


def _sigmoid(x):
    return 1.0 / (1.0 + jnp.exp(-x))


def _log_sigmoid(x):
    return jnp.minimum(x, 0.0) - jnp.log1p(jnp.exp(-jnp.abs(x)))


def _const_spec(shape):
    n = len(shape)
    return pl.BlockSpec(shape, lambda *_: (0,) * n)


def _mixer_kernel(*refs, chunk, n_chunks, carried, alpha):
    (x_ref, wmain_ref, wgc_ref, wgr_ref, bcol_ref, brow_ref, convw_ref, gn_ref, wout_ref,
     lng_ref, lnb_ref, cos_ref, sin_ref, dmask_ref, cd_ref, wkr_ref, gl_ref) = refs[:17]
    if carried:
        state_in = None
        rest = refs[17:]
    else:
        state_in = refs[17:22]
        rest = refs[22:]
    x1_ref, conv_out, c_out, n_out, m_out, s_out = rest[:6]
    proj_ref, gcol_ref, hbuf_ref, ub_ref, xh_ref = rest[6:]
    if carried:
        conv_in = None
        c_src, n_src, m_src, s_src = c_out, n_out, m_out, s_out
    else:
        conv_in, c_src, n_src, m_src, s_src = state_in
    L = chunk

    xb = x_ref[...]
    xh = _mx(xb)
    xh_ref[...] = xh
    proj_ref[...] = jnp.dot(xh, wmain_ref[...], preferred_element_type=F32)
    gcol_ref[...] = jnp.dot(xh, wgc_ref[...], preferred_element_type=F32) + bcol_ref[...]

    if carried:
        @pl.when(pl.program_id(1) == 0)
        def _():
            ub_ref[0:SUBLANES, :] = jnp.zeros((SUBLANES, D_MODEL), F32)
            c_out[...] = jnp.zeros(c_out.shape, F32)
            n_out[...] = jnp.zeros(n_out.shape, F32)
            m_out[...] = jnp.zeros(m_out.shape, F32)
            s_out[...] = jnp.zeros(s_out.shape, F32)

    row_i = lax.broadcasted_iota(jnp.int32, (L, L), 0)
    col_i = lax.broadcasted_iota(jnp.int32, (L, L), 1)
    causal = row_i >= col_i
    tri = causal.astype(F32)
    tri_t = (row_i <= col_i).astype(F32)
    convw = convw_ref[...]
    gn = gn_ref[...]

    def chunk_body(c, carry):
        r0 = pl.multiple_of(c * L, L)
        rows = pl.ds(r0, L)
        si = 0 if carried else c
        tab_rows = rows if carried else pl.ds(0, L)

        u = proj_ref[rows, 0:2 * M_WIDTH]
        if not carried:
            ub_ref[SUBLANES - 3:SUBLANES, :] = conv_in[si]
        ub_ref[SUBLANES:SUBLANES + L, :] = u
        acc = u * convw[CONV_W - 1:CONV_W, :]
        for j in range(CONV_W - 1):
            acc = acc + ub_ref[SUBLANES - 3 + j:SUBLANES - 3 + j + L, :] * convw[j:j + 1, :]
        last_rows = ub_ref[SUBLANES + L - 3:SUBLANES + L, :]
        conv_out[si] = last_rows
        if carried:
            ub_ref[SUBLANES - 3:SUBLANES, :] = last_rows
        qk = acc * _sigmoid(acc)

        gc = gcol_ref[rows, :]
        gr = _dot_nt(wgr_ref[...], xh_ref[rows, :]) + brow_ref[...]
        b_cols = _dot_exact(tri, _log_sigmoid(gc))
        b_rows = _dot_exact(_log_sigmoid(gr), tri_t)

        cos = cos_ref[tab_rows, :]
        sin = sin_ref[tab_rows, :]

        for h in range(M_HEADS):
            lo, hi = h * M_DIM, (h + 1) * M_DIM
            q = qk[:, lo:hi] * (M_DIM ** -0.5)
            k = qk[:, M_WIDTH + lo:M_WIDTH + hi]
            v = proj_ref[rows, 2 * M_WIDTH + lo:2 * M_WIDTH + hi]
            og = proj_ref[rows, 3 * M_WIDTH + lo:3 * M_WIDTH + hi]
            c0 = c_src[si, h]
            n0 = n_src[si, h:h + 1, :]
            m0 = m_src[si, h:h + 1, 0:1]
            b_c = b_cols[:, M_HEADS + h:M_HEADS + h + 1]
            b_r = b_rows[M_HEADS + h:M_HEADS + h + 1, :]
            ig_c = gc[:, h:h + 1]
            ig_r = gr[h:h + 1, :]
            a_c = b_c + m0
            dmat = jnp.where(causal, b_c - b_r + ig_r, -jnp.inf)
            m_c = jnp.maximum(a_c, jnp.max(dmat, axis=-1, keepdims=True))
            w_intra = jnp.exp(dmat - m_c)
            w_inter = jnp.exp(a_c - m_c)
            s = _dot_nt(q, k) * w_intra
            num = _dot(s, v) + w_inter * _dot(q, c0)
            den = jnp.sum(s, axis=-1, keepdims=True) + w_inter * jnp.sum(q * n0, axis=-1, keepdims=True)
            hm = num / jnp.maximum(jnp.abs(den), jnp.exp(-m_c))
            m_last = m_c[L - 1:L, :]
            wk_c = jnp.exp(b_c[L - 1:L, :] - b_c + ig_c - m_last)
            decay = jnp.exp(a_c[L - 1:L, :] - m_last)
            kw = wk_c * k
            c_out[si, h] = decay * c0 + _dot_tn(kw, v)
            n_out[si, h:h + 1, :] = decay * n0 + jnp.sum(kw, axis=0, keepdims=True)
            m_out[si, h:h + 1, :] = jnp.broadcast_to(m_last, (1, LANES))
            hm = _head_norm(hm) * gn[:, lo:hi] * _sigmoid(og)
            hbuf_ref[rows, lo:hi] = hm.astype(hbuf_ref.dtype)

        for h in range(R_HEADS):
            lo, hi = h * R_DIM, (h + 1) * R_DIM
            base = 4 * M_WIDTH
            rq = proj_ref[rows, base + lo:base + hi]
            rk = proj_ref[rows, base + R_WIDTH + lo:base + R_WIDTH + hi]
            v = proj_ref[rows, base + 2 * R_WIDTH + lo:base + 2 * R_WIDTH + hi]
            rg = proj_ref[rows, base + 3 * R_WIDTH + lo:base + 3 * R_WIDTH + hi]
            q = rq * cos + pltpu.roll(rq, R_DIM // 2, 1) * sin
            k = (rk * cos + pltpu.roll(rk, R_DIM // 2, 1) * sin) * (R_DIM ** -0.5)
            s0 = s_src[si, h]
            s = _dot_nt(q, k) * dmask_ref[h]
            hr = _dot(s, v) + _dot(q, s0) * cd_ref[h]
            s_out[si, h] = gl_ref[h] * s0 + _dot_tn(wkr_ref[h] * k, v)
            hr = _head_norm(hr) * gn[:, M_WIDTH + lo:M_WIDTH + hi] * (rg * _sigmoid(rg))
            hbuf_ref[rows, M_WIDTH + lo:M_WIDTH + hi] = hr.astype(hbuf_ref.dtype)
        return carry

    lax.fori_loop(0, n_chunks, chunk_body, 0)

    mix = jnp.dot(hbuf_ref[...], wout_ref[...], preferred_element_type=F32)
    x1_ref[...] = _layer_norm(alpha * xb + mix, lng_ref[...], lnb_ref[...])


def _retention_tables(L):
    log_g = jnp.log(1.0 - 2.0 ** (-5.0 - jnp.arange(R_HEADS, dtype=F32)))
    idx = jnp.arange(L, dtype=F32)
    rel = idx[:, None] - idx[None, :]
    dmask = jnp.where(rel >= 0, jnp.exp(log_g[:, None, None] * jnp.maximum(rel, 0.0)), 0.0)
    cd = jnp.exp(log_g[:, None] * (idx + 1.0))
    wk = jnp.exp(log_g[:, None] * (L - 1.0 - idx))
    gl = jnp.exp(log_g * L)
    bc = lambda t: jnp.broadcast_to(t[:, :, None], (R_HEADS, L, LANES))
    return dmask, bc(cd), bc(wk), jnp.broadcast_to(gl[:, None, None], (R_HEADS, 1, LANES))


def _rotary_tables(pos):
    half = R_DIM // 2
    inv = ROPE_BASE ** (-jnp.arange(half, dtype=F32) / half)
    ang = pos.astype(F32)[:, None] * inv[None, :]
    cos, sin = jnp.cos(ang), jnp.sin(ang)
    return jnp.concatenate([cos, cos], axis=-1), jnp.concatenate([-sin, sin], axis=-1)


def _mixer(x, lw, pos, state, alpha):
    S, T, D = x.shape
    carried = state is None
    if carried:
        block_rows = min(PROMPT_BLOCK_ROWS, T)
        L = min(PROMPT_CHUNK, block_rows)
        seqs = 1
        n_t = T // block_rows
        grid = (S, n_t)
    else:
        seqs = min(SAMPLE_SEQS_PER_BLOCK, S)
        L = T
        block_rows = seqs * T
        n_t = 1
        grid = (S // seqs, 1)
    n_chunks = block_rows // L
    xr = x.reshape(S * T, D)
    cos, sin = _rotary_tables(pos)
    dmask, cd, wkr, gl = _retention_tables(L)

    row_spec = pl.BlockSpec((block_rows, D), lambda b, t: (b * n_t + t, 0))
    tab_spec = pl.BlockSpec((block_rows if carried else T, LANES), lambda b, t: (t, 0))
    in_specs = [
        row_spec,
        _const_spec((D, MAIN_WIDTH)), _const_spec((D, LANES)), _const_spec((N_GATES, D)),
        _const_spec((1, LANES)), _const_spec((N_GATES, 1)), _const_spec((CONV_W, 2 * M_WIDTH)),
        _const_spec((1, D)), _const_spec((D, D)), _const_spec((1, D)), _const_spec((1, D)),
        tab_spec, tab_spec,
        _const_spec((R_HEADS, L, L)), _const_spec((R_HEADS, L, LANES)), _const_spec((R_HEADS, L, LANES)),
        _const_spec((R_HEADS, 1, LANES)),
    ]
    args = [xr, lw['w_main'], lw['w_gc'], lw['w_gr'], lw['b_col'], lw['b_row'], lw['conv_w'],
            lw['gn'], lw['w_out'], lw['ln_g'][0:1], lw['ln_b'][0:1], cos, sin, dmask, cd, wkr, gl]
    state_specs = [
        pl.BlockSpec((seqs, CONV_W - 1, 2 * M_WIDTH), lambda b, t: (b, 0, 0)),
        pl.BlockSpec((seqs, M_HEADS, M_DIM, M_DIM), lambda b, t: (b, 0, 0, 0)),
        pl.BlockSpec((seqs, M_HEADS, M_DIM), lambda b, t: (b, 0, 0)),
        pl.BlockSpec((seqs, M_HEADS, LANES), lambda b, t: (b, 0, 0)),
        pl.BlockSpec((seqs, R_HEADS, R_DIM, R_DIM), lambda b, t: (b, 0, 0, 0)),
    ]
    if not carried:
        in_specs += state_specs
        args += list(state)
    out_shape = [
        jax.ShapeDtypeStruct((S * T, D), F32),
        jax.ShapeDtypeStruct((S, CONV_W - 1, 2 * M_WIDTH), F32),
        jax.ShapeDtypeStruct((S, M_HEADS, M_DIM, M_DIM), F32),
        jax.ShapeDtypeStruct((S, M_HEADS, M_DIM), F32),
        jax.ShapeDtypeStruct((S, M_HEADS, LANES), F32),
        jax.ShapeDtypeStruct((S, R_HEADS, R_DIM, R_DIM), F32),
    ]
    out_specs = [row_spec] + state_specs
    scratch = [
        pltpu.VMEM((block_rows, MAIN_WIDTH), F32),
        pltpu.VMEM((block_rows, LANES), F32),
        pltpu.VMEM((block_rows, D), _MXU_DTYPE),
        pltpu.VMEM((SUBLANES + L, 2 * M_WIDTH), F32),
        pltpu.VMEM((block_rows, D), _MXU_DTYPE),
    ]
    outs = pl.pallas_call(
        functools.partial(_mixer_kernel, chunk=L, n_chunks=n_chunks, carried=carried, alpha=alpha),
        grid=grid, in_specs=in_specs, out_specs=out_specs, out_shape=out_shape,
        scratch_shapes=scratch,
        compiler_params=pltpu.CompilerParams(
            dimension_semantics=("parallel", "arbitrary"), vmem_limit_bytes=VMEM_LIMIT_BYTES),
        name="mixer_prompt" if carried else "mixer_sample",
    )(*args)
    x1, conv_new, c_new, n_new, m_new, s_new = outs
    return x1.reshape(S, T, D), conv_new, c_new, n_new, m_new[:, :, 0], s_new


def _combine_rows(logits_t):
    mx = jnp.max(logits_t, axis=0, keepdims=True)
    ex = jnp.exp(logits_t - mx)
    probs = ex / jnp.sum(ex, axis=0, keepdims=True)
    p = [probs[e:e + 1, :] for e in range(N_EXPERTS)]

    def first_max(vals):
        m = vals[0]
        for t in vals[1:]:
            m = jnp.maximum(m, t)
        taken = None
        sel = []
        for t in vals:
            is_max = t == m
            if taken is None:
                pick, taken = is_max, is_max
            else:
                pick = jnp.logical_and(is_max, jnp.logical_not(taken))
                taken = jnp.logical_or(taken, is_max)
            sel.append(pick)
        return m, sel

    scores, picked, denom = [], [], []
    for g in range(N_GROUPS):
        vals = p[g * EXP_PER_GROUP:(g + 1) * EXP_PER_GROUP]
        m1, sel1 = first_max(vals)
        rest = [jnp.where(s1, -1.0, t) for s1, t in zip(sel1, vals)]
        m2, sel2 = first_max(rest)
        scores.append(m1 + m2)
        denom.append(m1 + m2)
        picked.append([jnp.logical_or(a, b) for a, b in zip(sel1, sel2)])
    _, gsel = first_max(scores)
    out = []
    for g in range(N_GROUPS):
        for j in range(EXP_PER_GROUP):
            e = g * EXP_PER_GROUP + j
            out.append(jnp.where(jnp.logical_and(gsel[g], picked[g][j]), p[e] / denom[g], 0.0))
    return out


def _xattn_kernel(x_ref, k_ref, v_ref, wq_ref, wo_ref, lng_ref, lnb_ref, wr_ref,
                  x2_ref, comb_ref, q_ref, o_ref, *, seqs, seq_rows, alpha):
    xb = x_ref[...]
    q_ref[...] = _dot(xb, wq_ref[...]).astype(q_ref.dtype)
    scale = X_DIM ** -0.5
    for sq in range(seqs):
        r0 = sq * seq_rows
        for h in range(X_HEADS):
            lo, hi = h * X_DIM, (h + 1) * X_DIM
            s = _dot_nt(q_ref[r0:r0 + seq_rows, lo:hi], k_ref[sq, :, lo:hi]) * scale
            e = jnp.exp(s - jnp.max(s, axis=-1, keepdims=True))
            p = e / jnp.sum(e, axis=-1, keepdims=True)
            o_ref[r0:r0 + seq_rows, lo:hi] = _dot(p, v_ref[sq, :, lo:hi]).astype(o_ref.dtype)
    att = jnp.dot(o_ref[...], wo_ref[...], preferred_element_type=F32)
    x2 = _layer_norm(alpha * xb + att, lng_ref[...], lnb_ref[...])
    x2_ref[...] = x2
    logits_t = lax.dot_general(wr_ref[...], x2, (((1,), (1,)), ((), ())),
                               preferred_element_type=F32, precision=lax.Precision.HIGHEST)
    rows = _combine_rows(logits_t)
    n = xb.shape[0]
    comb_t = jnp.concatenate(rows + [jnp.zeros((LANES - N_EXPERTS, n), F32)], axis=0)
    comb_ref[...] = comb_t.T


def _xattn(x, mem_k, mem_v, lw, w_router_t, alpha):
    S, T, D = x.shape
    if T >= PROMPT_BLOCK_ROWS:
        seqs, seq_rows = 1, PROMPT_BLOCK_ROWS
    else:
        seqs, seq_rows = min(SAMPLE_SEQS_PER_BLOCK, S), T
    n_t = T // seq_rows
    block_rows = seqs * seq_rows
    grid = (S // seqs, n_t)
    row_map = lambda b, t: (b * n_t + t, 0)
    kv_spec = pl.BlockSpec((seqs, N_MEM, D), lambda b, t: (b, 0, 0))
    x2, comb = pl.pallas_call(
        functools.partial(_xattn_kernel, seqs=seqs, seq_rows=seq_rows, alpha=alpha),
        grid=grid,
        in_specs=[pl.BlockSpec((block_rows, D), row_map), kv_spec, kv_spec,
                  _const_spec((D, D)), _const_spec((D, D)), _const_spec((1, D)), _const_spec((1, D)),
                  _const_spec((N_EXPERTS, D))],
        out_specs=[pl.BlockSpec((block_rows, D), row_map), pl.BlockSpec((block_rows, LANES), row_map)],
        out_shape=[jax.ShapeDtypeStruct((S * T, D), F32), jax.ShapeDtypeStruct((S * T, LANES), F32)],
        scratch_shapes=[pltpu.VMEM((block_rows, D), _MXU_DTYPE), pltpu.VMEM((block_rows, D), _MXU_DTYPE)],
        compiler_params=pltpu.CompilerParams(
            dimension_semantics=("parallel", "arbitrary"), vmem_limit_bytes=VMEM_LIMIT_BYTES),
        name="xattn",
    )(x.reshape(S * T, D), mem_k, mem_v, lw['w_xq'], lw['w_xo'], lw['ln_g'][1:2], lw['ln_b'][1:2],
      w_router_t)
    return x2, comb


def _moe_kernel(x_ref, comb_ref, wg_ref, wu_ref, wd_ref, lng_ref, lnb_ref, out_ref, acc_ref, xh_ref,
                *, alpha):
    e = pl.program_id(1)

    @pl.when(e == 0)
    def _():
        acc_ref[...] = jnp.zeros(acc_ref.shape, F32)
        xh_ref[...] = _mx(x_ref[...])

    xh = xh_ref[...]
    gate = jnp.dot(xh, wg_ref[0], preferred_element_type=F32)
    up = jnp.dot(xh, wu_ref[0], preferred_element_type=F32)
    hmid = gate * _sigmoid(gate) * up
    y = _dot(hmid, wd_ref[0])
    comb = comb_ref[...]
    lane = lax.broadcasted_iota(jnp.int32, comb.shape, 1)
    c_e = jnp.sum(jnp.where(lane == e, comb, 0.0), axis=-1, keepdims=True)
    acc_ref[...] += c_e * y

    @pl.when(e == N_EXPERTS - 1)
    def _():
        out_ref[...] = _layer_norm(alpha * x_ref[...] + acc_ref[...], lng_ref[...], lnb_ref[...])


def _moe(x2, comb, lw, alpha):
    n, D = x2.shape
    rows = min(MOE_BLOCK_ROWS, n)
    row_map = lambda i, e: (i, 0)
    return pl.pallas_call(
        functools.partial(_moe_kernel, alpha=alpha),
        grid=(n // rows, N_EXPERTS),
        in_specs=[pl.BlockSpec((rows, D), row_map), pl.BlockSpec((rows, LANES), row_map),
                  pl.BlockSpec((1, D, D_EXPERT), lambda i, e: (e, 0, 0)),
                  pl.BlockSpec((1, D, D_EXPERT), lambda i, e: (e, 0, 0)),
                  pl.BlockSpec((1, D_EXPERT, D), lambda i, e: (e, 0, 0)),
                  _const_spec((1, D)), _const_spec((1, D))],
        out_specs=pl.BlockSpec((rows, D), row_map),
        out_shape=jax.ShapeDtypeStruct((n, D), F32),
        scratch_shapes=[pltpu.VMEM((rows, D), F32), pltpu.VMEM((rows, D), _MXU_DTYPE)],
        compiler_params=pltpu.CompilerParams(
            dimension_semantics=("parallel", "arbitrary"), vmem_limit_bytes=VMEM_LIMIT_BYTES),
        name="moe",
    )(x2, comb, lw['w_e_gate'], lw['w_e_up'], lw['w_e_down'], lw['ln_g'][2:3], lw['ln_b'][2:3])


def _memkv_kernel(mem_ref, wk_ref, wv_ref, k_ref, v_ref, kh_ref, vh_ref):
    mh = _mx(mem_ref[...])
    k = jnp.dot(mh, wk_ref[...], preferred_element_type=F32)
    v = jnp.dot(mh, wv_ref[...], preferred_element_type=F32)
    k_ref[...] = k
    v_ref[...] = v
    kh_ref[...] = k.astype(kh_ref.dtype)
    vh_ref[...] = v.astype(vh_ref.dtype)


def _mem_kv(mem, lw):
    n, D = mem.shape
    rows = min(PROMPT_BLOCK_ROWS, n)
    spec = pl.BlockSpec((rows, D), lambda i: (i, 0))
    return pl.pallas_call(
        _memkv_kernel,
        grid=(n // rows,),
        in_specs=[spec, _const_spec((D, D)), _const_spec((D, D))],
        out_specs=[spec, spec, spec, spec],
        out_shape=[jax.ShapeDtypeStruct((n, D), F32), jax.ShapeDtypeStruct((n, D), F32),
                   jax.ShapeDtypeStruct((n, D), _MXU_DTYPE), jax.ShapeDtypeStruct((n, D), _MXU_DTYPE)],
        compiler_params=pltpu.CompilerParams(
            dimension_semantics=("parallel",), vmem_limit_bytes=VMEM_LIMIT_BYTES),
        name="mem_kv",
    )(mem, lw['w_xk'], lw['w_xv'])


def _layer_weights(l, w_in, b_if, conv_w, gn_m, gn_r, w_out, w_xq, w_xk, w_xv, w_xo,
                   w_e_gate, w_e_up, w_e_down, ln_g, ln_b):
    wi = w_in[l]
    gates = wi[:, GATE_COL0:GATE_COL0 + N_GATES]
    b = b_if[l].astype(F32)
    return {
        'w_main': _mx(jnp.concatenate([wi[:, :GATE_COL0], wi[:, GATE_COL0 + N_GATES:]], axis=1)),
        'w_gc': _mx(jnp.pad(gates, ((0, 0), (0, LANES - N_GATES)))),
        'w_gr': _mx(gates.T),
        'b_col': jnp.pad(b, (0, LANES - N_GATES))[None, :],
        'b_row': b[:, None],
        'conv_w': conv_w[l].astype(F32),
        'gn': jnp.concatenate([gn_m[l], gn_r[l]]).astype(F32)[None, :],
        'w_out': _mx(w_out[l]), 'w_xq': _mx(w_xq[l]), 'w_xk': _mx(w_xk[l]), 'w_xv': _mx(w_xv[l]),
        'w_xo': _mx(w_xo[l]),
        'w_e_gate': _mx(w_e_gate[l]), 'w_e_up': _mx(w_e_up[l]), 'w_e_down': _mx(w_e_down[l]),
        'ln_g': ln_g[l].astype(F32), 'ln_b': ln_b[l].astype(F32),
    }


def kernel(x_prompt, x_sample, mem_prompt, cache_mem_k, cache_mem_v, state_conv, state_mlstm_C,
           state_mlstm_n, state_mlstm_m, state_ret_S, w_in, b_if, conv_w, gn_m, gn_r, w_out, w_xq,
           w_xk, w_xv, w_xo, w_router, w_e_gate, w_e_up, w_e_down, ln_g, ln_b):
    depth = w_in.shape[0]
    alpha = (2 * depth) ** 0.25
    Bp, Tp, D = x_prompt.shape
    Bs, Ts, _ = x_sample.shape
    pos_p = jnp.arange(Tp)
    pos_s = PAST_LEN + jnp.arange(Ts)
    w_router_t = w_router.astype(F32).T
    mem2d = mem_prompt.reshape(Bp * N_MEM, D)
    xp, xs = x_prompt, x_sample
    outs = [[] for _ in range(12)]
    for l in range(depth):
        lw = _layer_weights(l, w_in, b_if, conv_w, gn_m, gn_r, w_out, w_xq, w_xk, w_xv, w_xo,
                            w_e_gate, w_e_up, w_e_down, ln_g, ln_b)
        kp, vp, kph, vph = _mem_kv(mem2d, lw)
        x1, cb, c_n, n_n, m_n, s_n = _mixer(xp, lw, pos_p, None, alpha)
        x2, comb = _xattn(x1, kph.reshape(Bp, N_MEM, D), vph.reshape(Bp, N_MEM, D), lw, w_router_t, alpha)
        xp = _moe(x2, comb, lw, alpha).reshape(Bp, Tp, D)
        for lst, val in zip(outs[:7], (kp.reshape(Bp, N_MEM, X_HEADS, X_DIM),
                                       vp.reshape(Bp, N_MEM, X_HEADS, X_DIM), cb, c_n, n_n, m_n, s_n)):
            lst.append(val)
        state = (state_conv[l], state_mlstm_C[l], state_mlstm_n[l],
                 jnp.broadcast_to(state_mlstm_m[l][:, :, None], (Bs, M_HEADS, LANES)), state_ret_S[l])
        x1, cb, c_n, n_n, m_n, s_n = _mixer(xs, lw, pos_s, state, alpha)
        x2, comb = _xattn(x1, _mx(cache_mem_k[l].reshape(Bs, N_MEM, D)),
                          _mx(cache_mem_v[l].reshape(Bs, N_MEM, D)), lw, w_router_t, alpha)
        xs = _moe(x2, comb, lw, alpha).reshape(Bs, Ts, D)
        for lst, val in zip(outs[7:], (cb, c_n, n_n, m_n, s_n)):
            lst.append(val)
    return (xp, xs) + tuple(jnp.stack(o) for o in outs)
```

```python
import functools

import jax
import jax.numpy as jnp
from jax import lax
from jax.experimental import pallas as pl
from jax.experimental.pallas import tpu as pltpu

F32 = jnp.float32
_MXU_DTYPE = jnp.bfloat16

D_MODEL = 1024
PAST_LEN = 4096
N_MEM = 256
M_WIDTH = D_MODEL // 2
M_HEADS = 4
M_DIM = M_WIDTH // M_HEADS
R_WIDTH = D_MODEL - M_WIDTH
R_HEADS = 4
R_DIM = R_WIDTH // R_HEADS
CONV_W = 4
ROPE_BASE = 10000.0
X_HEADS = 4
X_DIM = D_MODEL // X_HEADS
N_EXPERTS = 16
N_GROUPS = 4
EXP_PER_GROUP = N_EXPERTS // N_GROUPS
D_EXPERT = D_MODEL // 2
LN_EPS = 1e-5
GATE_COL0 = 2 * M_WIDTH + 2 * M_WIDTH
N_GATES = 2 * M_HEADS
MAIN_WIDTH = 4 * D_MODEL

LANES = 128
SUBLANES = 8
VMEM_LIMIT_BYTES = 56 * 1024 * 1024

PROMPT_BLOCK_ROWS = 512
PROMPT_CHUNK = 256
SAMPLE_SEQS_PER_BLOCK = 8

PAIRS = ((0, 1), (0, 2), (0, 3), (1, 2), (1, 3), (2, 3))
N_BUCKETS = N_GROUPS * len(PAIRS)
BUCKET_ROWS = 32
FEATURE_ROWS = D_MODEL // LANES
TOKEN_TILE_ROWS = 2 * SUBLANES
EXPERT_TILE = 256
DMA_UNROLL = 8
ZERO_ROWS = EXPERT_TILE // 2


def _mx(a):
    return a.astype(_MXU_DTYPE)


def _dot(a, b):
    return jnp.dot(_mx(a), _mx(b), preferred_element_type=F32)


def _dot_nt(a, b):
    return lax.dot_general(_mx(a), _mx(b), (((1,), (1,)), ((), ())), preferred_element_type=F32)


def _dot_tn(a, b):
    return lax.dot_general(_mx(a), _mx(b), (((0,), (0,)), ((), ())), preferred_element_type=F32)


def _dot_exact(a, b):
    return jnp.dot(a, b, preferred_element_type=F32, precision=lax.Precision.HIGHEST)


def _layer_norm(y, g, b):
    mu = jnp.mean(y, axis=-1, keepdims=True)
    yc = y - mu
    var = jnp.mean(yc * yc, axis=-1, keepdims=True)
    return yc * lax.rsqrt(var + LN_EPS) * g + b


def _head_norm(h):
    mu = jnp.mean(h, axis=-1, keepdims=True)
    hc = h - mu
    var = jnp.mean(hc * hc, axis=-1, keepdims=True)
    return hc * lax.rsqrt(var + LN_EPS)


def _sigmoid(x):
    return 1.0 / (1.0 + jnp.exp(-x))


def _log_sigmoid(x):
    return jnp.minimum(x, 0.0) - jnp.log1p(jnp.exp(-jnp.abs(x)))


def _const_spec(shape):
    n = len(shape)
    return pl.BlockSpec(shape, lambda *_: (0,) * n)


def _mixer_kernel(*refs, chunk, n_chunks, carried, alpha):
    (x_ref, wmain_ref, wgc_ref, wgr_ref, bcol_ref, brow_ref, convw_ref, gn_ref, wout_ref,
     lng_ref, lnb_ref, cos_ref, sin_ref, dmask_ref, cd_ref, wkr_ref, gl_ref) = refs[:17]
    if carried:
        state_in = None
        rest = refs[17:]
    else:
        state_in = refs[17:22]
        rest = refs[22:]
    x1_ref, conv_out, c_out, n_out, m_out, s_out = rest[:6]
    proj_ref, gcol_ref, hbuf_ref, ub_ref, xh_ref = rest[6:]
    if carried:
        conv_in = None
        c_src, n_src, m_src, s_src = c_out, n_out, m_out, s_out
    else:
        conv_in, c_src, n_src, m_src, s_src = state_in
    L = chunk

    xb = x_ref[...]
    xh = _mx(xb)
    xh_ref[...] = xh
    proj_ref[...] = jnp.dot(xh, wmain_ref[...], preferred_element_type=F32)
    gcol_ref[...] = jnp.dot(xh, wgc_ref[...], preferred_element_type=F32) + bcol_ref[...]

    if carried:
        @pl.when(pl.program_id(1) == 0)
        def _():
            ub_ref[0:SUBLANES, :] = jnp.zeros((SUBLANES, D_MODEL), F32)
            c_out[...] = jnp.zeros(c_out.shape, F32)
            n_out[...] = jnp.zeros(n_out.shape, F32)
            m_out[...] = jnp.zeros(m_out.shape, F32)
            s_out[...] = jnp.zeros(s_out.shape, F32)

    row_i = lax.broadcasted_iota(jnp.int32, (L, L), 0)
    col_i = lax.broadcasted_iota(jnp.int32, (L, L), 1)
    causal = row_i >= col_i
    tri = causal.astype(F32)
    tri_t = (row_i <= col_i).astype(F32)
    convw = convw_ref[...]
    gn = gn_ref[...]

    def chunk_body(c, carry):
        r0 = pl.multiple_of(c * L, L)
        rows = pl.ds(r0, L)
        si = 0 if carried else c
        tab_rows = rows if carried else pl.ds(0, L)

        u = proj_ref[rows, 0:2 * M_WIDTH]
        if not carried:
            ub_ref[SUBLANES - 3:SUBLANES, :] = conv_in[si]
        ub_ref[SUBLANES:SUBLANES + L, :] = u
        acc = u * convw[CONV_W - 1:CONV_W, :]
        for j in range(CONV_W - 1):
            acc = acc + ub_ref[SUBLANES - 3 + j:SUBLANES - 3 + j + L, :] * convw[j:j + 1, :]
        last_rows = ub_ref[SUBLANES + L - 3:SUBLANES + L, :]
        conv_out[si] = last_rows
        if carried:
            ub_ref[SUBLANES - 3:SUBLANES, :] = last_rows
        qk = acc * _sigmoid(acc)

        gc = gcol_ref[rows, :]
        gr = _dot_nt(wgr_ref[...], xh_ref[rows, :]) + brow_ref[...]
        b_cols = _dot_exact(tri, _log_sigmoid(gc))
        b_rows = _dot_exact(_log_sigmoid(gr), tri_t)

        cos = cos_ref[tab_rows, :]
        sin = sin_ref[tab_rows, :]

        for h in range(M_HEADS):
            lo, hi = h * M_DIM, (h + 1) * M_DIM
            q = qk[:, lo:hi] * (M_DIM ** -0.5)
            k = qk[:, M_WIDTH + lo:M_WIDTH + hi]
            v = proj_ref[rows, 2 * M_WIDTH + lo:2 * M_WIDTH + hi]
            og = proj_ref[rows, 3 * M_WIDTH + lo:3 * M_WIDTH + hi]
            c0 = c_src[si, h]
            n0 = n_src[si, h:h + 1, :]
            m0 = m_src[si, h:h + 1, 0:1]
            b_c = b_cols[:, M_HEADS + h:M_HEADS + h + 1]
            b_r = b_rows[M_HEADS + h:M_HEADS + h + 1, :]
            ig_c = gc[:, h:h + 1]
            ig_r = gr[h:h + 1, :]
            a_c = b_c + m0
            dmat = jnp.where(causal, b_c - b_r + ig_r, -jnp.inf)
            m_c = jnp.maximum(a_c, jnp.max(dmat, axis=-1, keepdims=True))
            w_intra = jnp.exp(dmat - m_c)
            w_inter = jnp.exp(a_c - m_c)
            s = _dot_nt(q, k) * w_intra
            num = _dot(s, v) + w_inter * _dot(q, c0)
            den = jnp.sum(s, axis=-1, keepdims=True) + w_inter * jnp.sum(q * n0, axis=-1, keepdims=True)
            hm = num / jnp.maximum(jnp.abs(den), jnp.exp(-m_c))
            m_last = m_c[L - 1:L, :]
            wk_c = jnp.exp(b_c[L - 1:L, :] - b_c + ig_c - m_last)
            decay = jnp.exp(a_c[L - 1:L, :] - m_last)
            kw = wk_c * k
            c_out[si, h] = decay * c0 + _dot_tn(kw, v)
            n_out[si, h:h + 1, :] = decay * n0 + jnp.sum(kw, axis=0, keepdims=True)
            m_out[si, h:h + 1, :] = jnp.broadcast_to(m_last, (1, LANES))
            hm = _head_norm(hm) * gn[:, lo:hi] * _sigmoid(og)
            hbuf_ref[rows, lo:hi] = hm.astype(hbuf_ref.dtype)

        for h in range(R_HEADS):
            lo, hi = h * R_DIM, (h + 1) * R_DIM
            base = 4 * M_WIDTH
            rq = proj_ref[rows, base + lo:base + hi]
            rk = proj_ref[rows, base + R_WIDTH + lo:base + R_WIDTH + hi]
            v = proj_ref[rows, base + 2 * R_WIDTH + lo:base + 2 * R_WIDTH + hi]
            rg = proj_ref[rows, base + 3 * R_WIDTH + lo:base + 3 * R_WIDTH + hi]
            q = rq * cos + pltpu.roll(rq, R_DIM // 2, 1) * sin
            k = (rk * cos + pltpu.roll(rk, R_DIM // 2, 1) * sin) * (R_DIM ** -0.5)
            s0 = s_src[si, h]
            s = _dot_nt(q, k) * dmask_ref[h]
            hr = _dot(s, v) + _dot(q, s0) * cd_ref[h]
            s_out[si, h] = gl_ref[h] * s0 + _dot_tn(wkr_ref[h] * k, v)
            hr = _head_norm(hr) * gn[:, M_WIDTH + lo:M_WIDTH + hi] * (rg * _sigmoid(rg))
            hbuf_ref[rows, M_WIDTH + lo:M_WIDTH + hi] = hr.astype(hbuf_ref.dtype)
        return carry

    lax.fori_loop(0, n_chunks, chunk_body, 0)

    mix = jnp.dot(hbuf_ref[...], wout_ref[...], preferred_element_type=F32)
    x1_ref[...] = _layer_norm(alpha * xb + mix, lng_ref[...], lnb_ref[...])


def _retention_tables(L):
    log_g = jnp.log(1.0 - 2.0 ** (-5.0 - jnp.arange(R_HEADS, dtype=F32)))
    idx = jnp.arange(L, dtype=F32)
    rel = idx[:, None] - idx[None, :]
    dmask = jnp.where(rel >= 0, jnp.exp(log_g[:, None, None] * jnp.maximum(rel, 0.0)), 0.0)
    cd = jnp.exp(log_g[:, None] * (idx + 1.0))
    wk = jnp.exp(log_g[:, None] * (L - 1.0 - idx))
    gl = jnp.exp(log_g * L)
    bc = lambda t: jnp.broadcast_to(t[:, :, None], (R_HEADS, L, LANES))
    return dmask, bc(cd), bc(wk), jnp.broadcast_to(gl[:, None, None], (R_HEADS, 1, LANES))


def _rotary_tables(pos):
    half = R_DIM // 2
    inv = ROPE_BASE ** (-jnp.arange(half, dtype=F32) / half)
    ang = pos.astype(F32)[:, None] * inv[None, :]
    cos, sin = jnp.cos(ang), jnp.sin(ang)
    return jnp.concatenate([cos, cos], axis=-1), jnp.concatenate([-sin, sin], axis=-1)


def _mixer(x, lw, pos, state, alpha):
    S, T, D = x.shape
    carried = state is None
    if carried:
        block_rows = min(PROMPT_BLOCK_ROWS, T)
        L = min(PROMPT_CHUNK, block_rows)
        seqs = 1
        n_t = T // block_rows
        grid = (S, n_t)
    else:
        seqs = min(SAMPLE_SEQS_PER_BLOCK, S)
        L = T
        block_rows = seqs * T
        n_t = 1
        grid = (S // seqs, 1)
    n_chunks = block_rows // L
    xr = x.reshape(S * T, D)
    cos, sin = _rotary_tables(pos)
    dmask, cd, wkr, gl = _retention_tables(L)

    row_spec = pl.BlockSpec((block_rows, D), lambda b, t: (b * n_t + t, 0))
    tab_spec = pl.BlockSpec((block_rows if carried else T, LANES), lambda b, t: (t, 0))
    in_specs = [
        row_spec,
        _const_spec((D, MAIN_WIDTH)), _const_spec((D, LANES)), _const_spec((N_GATES, D)),
        _const_spec((1, LANES)), _const_spec((N_GATES, 1)), _const_spec((CONV_W, 2 * M_WIDTH)),
        _const_spec((1, D)), _const_spec((D, D)), _const_spec((1, D)), _const_spec((1, D)),
        tab_spec, tab_spec,
        _const_spec((R_HEADS, L, L)), _const_spec((R_HEADS, L, LANES)), _const_spec((R_HEADS, L, LANES)),
        _const_spec((R_HEADS, 1, LANES)),
    ]
    args = [xr, lw['w_main'], lw['w_gc'], lw['w_gr'], lw['b_col'], lw['b_row'], lw['conv_w'],
            lw['gn'], lw['w_out'], lw['ln_g'][0:1], lw['ln_b'][0:1], cos, sin, dmask, cd, wkr, gl]
    state_specs = [
        pl.BlockSpec((seqs, CONV_W - 1, 2 * M_WIDTH), lambda b, t: (b, 0, 0)),
        pl.BlockSpec((seqs, M_HEADS, M_DIM, M_DIM), lambda b, t: (b, 0, 0, 0)),
        pl.BlockSpec((seqs, M_HEADS, M_DIM), lambda b, t: (b, 0, 0)),
        pl.BlockSpec((seqs, M_HEADS, LANES), lambda b, t: (b, 0, 0)),
        pl.BlockSpec((seqs, R_HEADS, R_DIM, R_DIM), lambda b, t: (b, 0, 0, 0)),
    ]
    if not carried:
        in_specs += state_specs
        args += list(state)
    out_shape = [
        jax.ShapeDtypeStruct((S * T, D), F32),
        jax.ShapeDtypeStruct((S, CONV_W - 1, 2 * M_WIDTH), F32),
        jax.ShapeDtypeStruct((S, M_HEADS, M_DIM, M_DIM), F32),
        jax.ShapeDtypeStruct((S, M_HEADS, M_DIM), F32),
        jax.ShapeDtypeStruct((S, M_HEADS, LANES), F32),
        jax.ShapeDtypeStruct((S, R_HEADS, R_DIM, R_DIM), F32),
    ]
    out_specs = [row_spec] + state_specs
    scratch = [
        pltpu.VMEM((block_rows, MAIN_WIDTH), F32),
        pltpu.VMEM((block_rows, LANES), F32),
        pltpu.VMEM((block_rows, D), _MXU_DTYPE),
        pltpu.VMEM((SUBLANES + L, 2 * M_WIDTH), F32),
        pltpu.VMEM((block_rows, D), _MXU_DTYPE),
    ]
    outs = pl.pallas_call(
        functools.partial(_mixer_kernel, chunk=L, n_chunks=n_chunks, carried=carried, alpha=alpha),
        grid=grid, in_specs=in_specs, out_specs=out_specs, out_shape=out_shape,
        scratch_shapes=scratch,
        compiler_params=pltpu.CompilerParams(
            dimension_semantics=("parallel", "arbitrary"), vmem_limit_bytes=VMEM_LIMIT_BYTES),
        name="mixer_prompt" if carried else "mixer_sample",
    )(*args)
    x1, conv_new, c_new, n_new, m_new, s_new = outs
    return x1.reshape(S, T, D), conv_new, c_new, n_new, m_new[:, :, 0], s_new


def _route_rows(logits_t):
    mx = jnp.max(logits_t, axis=0, keepdims=True)
    ex = jnp.exp(logits_t - mx)
    probs = ex / jnp.sum(ex, axis=0, keepdims=True)
    p = [probs[e:e + 1, :] for e in range(N_EXPERTS)]

    def first_max(vals):
        m = vals[0]
        for t in vals[1:]:
            m = jnp.maximum(m, t)
        taken = None
        sel = []
        for t in vals:
            is_max = t == m
            if taken is None:
                pick, taken = is_max, is_max
            else:
                pick = jnp.logical_and(is_max, jnp.logical_not(taken))
                taken = jnp.logical_or(taken, is_max)
            sel.append(pick)
        return m, sel

    scores, picked, denom = [], [], []
    for g in range(N_GROUPS):
        vals = p[g * EXP_PER_GROUP:(g + 1) * EXP_PER_GROUP]
        m1, sel1 = first_max(vals)
        rest = [jnp.where(s1, -1.0, t) for s1, t in zip(sel1, vals)]
        m2, sel2 = first_max(rest)
        scores.append(m1 + m2)
        denom.append(m1 + m2)
        picked.append([jnp.logical_or(a, b) for a, b in zip(sel1, sel2)])
    _, gsel = first_max(scores)
    bucket = jnp.zeros(p[0].shape, jnp.int32)
    ca = jnp.zeros(p[0].shape, F32)
    cb = jnp.zeros(p[0].shape, F32)
    for g in range(N_GROUPS):
        w = [p[g * EXP_PER_GROUP + j] / denom[g] for j in range(EXP_PER_GROUP)]
        for pi, (a, b) in enumerate(PAIRS):
            hit = jnp.logical_and(gsel[g], jnp.logical_and(picked[g][a], picked[g][b]))
            bucket = jnp.where(hit, g * len(PAIRS) + pi, bucket)
            ca = jnp.where(hit, w[a], ca)
            cb = jnp.where(hit, w[b], cb)
    return bucket, ca, cb


def _xattn_kernel(x_ref, k_ref, v_ref, wq_ref, wo_ref, lng_ref, lnb_ref, wr_ref, upper_ref, cnt_in_ref,
                  x2_ref, tok_ref, rb_ref, cnt_out_ref, q_ref, o_ref, run_ref, *, seqs, seq_rows, alpha):
    @pl.when(jnp.logical_and(pl.program_id(0) == 0, pl.program_id(1) == 0))
    def _():
        run_ref[...] = cnt_in_ref[...]

    xb = x_ref[...]
    q_ref[...] = _dot(xb, wq_ref[...]).astype(q_ref.dtype)
    scale = X_DIM ** -0.5
    for sq in range(seqs):
        r0 = sq * seq_rows
        for h in range(X_HEADS):
            lo, hi = h * X_DIM, (h + 1) * X_DIM
            s = _dot_nt(q_ref[r0:r0 + seq_rows, lo:hi], k_ref[sq, :, lo:hi]) * scale
            e = jnp.exp(s - jnp.max(s, axis=-1, keepdims=True))
            p = e / jnp.sum(e, axis=-1, keepdims=True)
            o_ref[r0:r0 + seq_rows, lo:hi] = _dot(p, v_ref[sq, :, lo:hi]).astype(o_ref.dtype)
    att = jnp.dot(o_ref[...], wo_ref[...], preferred_element_type=F32)
    x2 = _layer_norm(alpha * xb + att, lng_ref[...], lnb_ref[...])
    x2_ref[...] = x2
    n = xb.shape[0]

    logits_t = lax.dot_general(wr_ref[...], x2, (((1,), (1,)), ((), ())),
                               preferred_element_type=F32, precision=lax.Precision.HIGHEST)
    bucket, ca, cb = _route_rows(logits_t)

    onehot = (lax.broadcasted_iota(jnp.int32, (BUCKET_ROWS, n), 0) == bucket).astype(F32)
    earlier = jnp.dot(_mx(onehot), upper_ref[...], preferred_element_type=F32)
    run = run_ref[...]
    rank = jnp.sum(onehot * (earlier + run[:, 0:1]), axis=0, keepdims=True)
    run_ref[...] = run + jnp.sum(onehot, axis=1, keepdims=True)
    cnt_out_ref[...] = run_ref[...]
    rb_ref[...] = jnp.concatenate(
        [bucket, rank.astype(jnp.int32), jnp.zeros((SUBLANES - 2, n), jnp.int32)], axis=0)

    tok_ref[...] = jnp.zeros(tok_ref.shape, F32)
    for s in range(FEATURE_ROWS):
        tok_ref[:, s, :] = x2[:, s * LANES:(s + 1) * LANES]
    cw_t = jnp.concatenate([ca, cb, jnp.zeros((LANES - 2, n), F32)], axis=0)
    tok_ref[:, FEATURE_ROWS, :] = cw_t.T


def _xattn(x, mem_k, mem_v, lw, w_router_t, cnt_in, alpha):
    S, T, D = x.shape
    if T >= PROMPT_BLOCK_ROWS:
        seqs, seq_rows = 1, PROMPT_BLOCK_ROWS
    else:
        seqs, seq_rows = min(SAMPLE_SEQS_PER_BLOCK, S), T
    n_t = T // seq_rows
    block_rows = seqs * seq_rows
    grid = (S // seqs, n_t)
    row_map = lambda b, t: (b * n_t + t, 0)
    kv_spec = pl.BlockSpec((seqs, N_MEM, D), lambda b, t: (b, 0, 0))
    idx = jnp.arange(block_rows)
    upper = _mx(idx[:, None] < idx[None, :])
    return pl.pallas_call(
        functools.partial(_xattn_kernel, seqs=seqs, seq_rows=seq_rows, alpha=alpha),
        grid=grid,
        in_specs=[pl.BlockSpec((block_rows, D), row_map), kv_spec, kv_spec,
                  _const_spec((D, D)), _const_spec((D, D)), _const_spec((1, D)), _const_spec((1, D)),
                  _const_spec((N_EXPERTS, D)), _const_spec((block_rows, block_rows)),
                  _const_spec((BUCKET_ROWS, LANES))],
        out_specs=[pl.BlockSpec((block_rows, D), row_map),
                   pl.BlockSpec((block_rows, TOKEN_TILE_ROWS, LANES), lambda b, t: (b * n_t + t, 0, 0)),
                   pl.BlockSpec((SUBLANES, block_rows), lambda b, t: (0, b * n_t + t)),
                   _const_spec((BUCKET_ROWS, LANES))],
        out_shape=[jax.ShapeDtypeStruct((S * T, D), F32),
                   jax.ShapeDtypeStruct((S * T, TOKEN_TILE_ROWS, LANES), F32),
                   jax.ShapeDtypeStruct((SUBLANES, S * T), jnp.int32),
                   jax.ShapeDtypeStruct((BUCKET_ROWS, LANES), F32)],
        scratch_shapes=[pltpu.VMEM((block_rows, D), _MXU_DTYPE), pltpu.VMEM((block_rows, D), _MXU_DTYPE),
                        pltpu.VMEM((BUCKET_ROWS, LANES), F32)],
        compiler_params=pltpu.CompilerParams(
            dimension_semantics=("arbitrary", "arbitrary"), vmem_limit_bytes=VMEM_LIMIT_BYTES),
        name="xattn",
    )(x.reshape(S * T, D), mem_k, mem_v, lw['w_xq'], lw['w_xo'], lw['ln_g'][1:2], lw['ln_b'][1:2],
      w_router_t, upper, cnt_in)


def _route_tables(cnt, n_tiles):
    cnt = cnt[:N_BUCKETS, 0].astype(jnp.int32)
    padded = ((cnt + EXPERT_TILE - 1) // EXPERT_TILE) * EXPERT_TILE
    ends = jnp.cumsum(padded)
    base = ends - padded
    tile_start = jnp.arange(n_tiles, dtype=jnp.int32) * EXPERT_TILE
    valid = tile_start < ends[-1]
    last_start = jnp.maximum(ends[-1] - EXPERT_TILE, 0)
    tb = jnp.searchsorted(ends, jnp.where(valid, tile_start, last_start), side='right')
    tb = jnp.minimum(tb, N_BUCKETS - 1).astype(jnp.int32)
    group, pair = tb // len(PAIRS), tb % len(PAIRS)
    pair_a = jnp.asarray([a for a, _ in PAIRS], jnp.int32)
    pair_b = jnp.asarray([b for _, b in PAIRS], jnp.int32)
    ea = group * EXP_PER_GROUP + pair_a[pair]
    eb = group * EXP_PER_GROUP + pair_b[pair]
    used_tiles = (ends[-1:] // EXPERT_TILE).astype(jnp.int32)
    return base, base + cnt, padded - cnt, used_tiles, ea, eb, valid.astype(jnp.int32)


def _dispatch_kernel(pos_p_ref, pos_s_ref, pad_start_ref, pad_len_ref, used_tiles_ref, tok_p_ref, tok_s_ref,
                     xs_ref, zero_ref, sem, pad_sem):
    zero_ref[...] = jnp.zeros(zero_ref.shape, F32)
    n_tiles = xs_ref.shape[0] // EXPERT_TILE

    def unused_tile(t, wait):
        cp = pltpu.make_async_copy(zero_ref, xs_ref.at[pl.ds(t * EXPERT_TILE, EXPERT_TILE)], pad_sem)
        cp.wait() if wait else cp.start()

    lax.fori_loop(used_tiles_ref[0], n_tiles, lambda t, c: (unused_tile(t, False), c)[1], 0)

    def rows(tok_ref, pos_ref):
        n = tok_ref.shape[0]

        def issue(i, carry):
            for j in range(DMA_UNROLL):
                r = i * DMA_UNROLL + j
                pltpu.make_async_copy(tok_ref.at[r], xs_ref.at[pos_ref[r]], sem).start()
            return carry
        lax.fori_loop(0, n // DMA_UNROLL, issue, 0)

    rows(tok_p_ref, pos_p_ref)
    rows(tok_s_ref, pos_s_ref)

    def pad_copies(b, wait):
        off = pad_start_ref[b]
        left = pad_len_ref[b]
        size = ZERO_ROWS
        while size >= 1:
            cp = pltpu.make_async_copy(zero_ref.at[pl.ds(0, size)], xs_ref.at[pl.ds(off, size)], pad_sem)

            @pl.when(left >= size)
            def _():
                cp.wait() if wait else cp.start()
            take = left >= size
            off = jnp.where(take, off + size, off)
            left = jnp.where(take, left - size, left)
            size //= 2

    def pad_start(b, carry):
        pad_copies(b, False)
        return carry

    def pad_wait(b, carry):
        pad_copies(b, True)
        return carry
    lax.fori_loop(0, N_BUCKETS, pad_start, 0)
    lax.fori_loop(0, N_BUCKETS, pad_wait, 0)
    lax.fori_loop(used_tiles_ref[0], n_tiles, lambda t, c: (unused_tile(t, True), c)[1], 0)

    def wait_rows(tok_ref):
        n = tok_ref.shape[0]
        w = min(n, EXPERT_TILE)

        def wait(i, carry):
            pltpu.make_async_copy(tok_ref.at[pl.ds(0, w)], xs_ref.at[pl.ds(0, w)], sem).wait()
            return carry
        lax.fori_loop(0, n // w, wait, 0)
    wait_rows(tok_p_ref)
    wait_rows(tok_s_ref)


def _dispatch(tok_p, tok_s, pos_p, pos_s, pad_start, pad_len, used_tiles, n_rows):
    any_spec = pl.BlockSpec(memory_space=pl.ANY)
    return pl.pallas_call(
        _dispatch_kernel,
        grid_spec=pltpu.PrefetchScalarGridSpec(
            num_scalar_prefetch=5, grid=(1,), in_specs=[any_spec, any_spec], out_specs=any_spec,
            scratch_shapes=[pltpu.VMEM((EXPERT_TILE, TOKEN_TILE_ROWS, LANES), F32),
                            pltpu.SemaphoreType.DMA, pltpu.SemaphoreType.DMA]),
        out_shape=jax.ShapeDtypeStruct((n_rows, TOKEN_TILE_ROWS, LANES), F32),
        compiler_params=pltpu.CompilerParams(
            dimension_semantics=("arbitrary",), vmem_limit_bytes=VMEM_LIMIT_BYTES),
        name="dispatch",
    )(pos_p, pos_s, pad_start, pad_len, used_tiles, tok_p, tok_s)


def _expert_kernel(ea_ref, eb_ref, valid_ref, xs_ref, wga_ref, wua_ref, wda_ref, wgb_ref, wub_ref, wdb_ref,
                   ys_ref, ga_ref, ua_ref, da_ref, gb_ref, ub_ref, db_ref):
    t = pl.program_id(0)

    @pl.when(valid_ref[t] > 0)
    def _():
        prev = jnp.maximum(t - 1, 0)

        @pl.when(jnp.logical_or(t == 0, ea_ref[t] != ea_ref[prev]))
        def _():
            ga_ref[...] = _mx(wga_ref[0])
            ua_ref[...] = _mx(wua_ref[0])
            da_ref[...] = _mx(wda_ref[0])

        @pl.when(jnp.logical_or(t == 0, eb_ref[t] != eb_ref[prev]))
        def _():
            gb_ref[...] = _mx(wgb_ref[0])
            ub_ref[...] = _mx(wub_ref[0])
            db_ref[...] = _mx(wdb_ref[0])

        x = _mx(jnp.concatenate([xs_ref[:, s, :] for s in range(FEATURE_ROWS)], axis=-1))
        cw = xs_ref[:, FEATURE_ROWS, :]

        def ffn(g_ref, u_ref, d_ref):
            gate = jnp.dot(x, g_ref[...], preferred_element_type=F32)
            up = jnp.dot(x, u_ref[...], preferred_element_type=F32)
            return _dot(gate * _sigmoid(gate) * up, d_ref[...])
        y = cw[:, 0:1] * ffn(ga_ref, ua_ref, da_ref) + cw[:, 1:2] * ffn(gb_ref, ub_ref, db_ref)
        for s in range(FEATURE_ROWS):
            ys_ref[:, s, :] = y[:, s * LANES:(s + 1) * LANES]

    @pl.when(valid_ref[t] == 0)
    def _():
        ys_ref[...] = jnp.zeros(ys_ref.shape, F32)


def _experts(xs, ea, eb, valid, w_e_gate, w_e_up, w_e_down):
    n_tiles = ea.shape[0]
    D = D_MODEL
    wa = lambda t, ea, eb, v: (ea[t], 0, 0)
    wb = lambda t, ea, eb, v: (eb[t], 0, 0)
    gu = (1, D, D_EXPERT)
    dn = (1, D_EXPERT, D)
    return pl.pallas_call(
        _expert_kernel,
        grid_spec=pltpu.PrefetchScalarGridSpec(
            num_scalar_prefetch=3, grid=(n_tiles,),
            in_specs=[pl.BlockSpec((EXPERT_TILE, TOKEN_TILE_ROWS, LANES),
                                   lambda t, ea, eb, v: (t * v[t], 0, 0)),
                      pl.BlockSpec(gu, wa), pl.BlockSpec(gu, wa), pl.BlockSpec(dn, wa),
                      pl.BlockSpec(gu, wb), pl.BlockSpec(gu, wb), pl.BlockSpec(dn, wb)],
            out_specs=pl.BlockSpec((EXPERT_TILE, FEATURE_ROWS, LANES), lambda t, ea, eb, v: (t, 0, 0)),
            scratch_shapes=[pltpu.VMEM((D, D_EXPERT), _MXU_DTYPE), pltpu.VMEM((D, D_EXPERT), _MXU_DTYPE),
                            pltpu.VMEM((D_EXPERT, D), _MXU_DTYPE), pltpu.VMEM((D, D_EXPERT), _MXU_DTYPE),
                            pltpu.VMEM((D, D_EXPERT), _MXU_DTYPE), pltpu.VMEM((D_EXPERT, D), _MXU_DTYPE)]),
        out_shape=jax.ShapeDtypeStruct((n_tiles * EXPERT_TILE, FEATURE_ROWS, LANES), F32),
        compiler_params=pltpu.CompilerParams(
            dimension_semantics=("arbitrary",), vmem_limit_bytes=VMEM_LIMIT_BYTES),
        name="experts",
    )(ea, eb, valid, xs, w_e_gate, w_e_up, w_e_down, w_e_gate, w_e_up, w_e_down)


def _combine_kernel(pos_ref, x_ref, ys_ref, lng_ref, lnb_ref, out_ref, buf_ref, sem, *, alpha):
    i = pl.program_id(0)
    n_blocks = pl.num_programs(0)
    rows = x_ref.shape[0]

    def fetch(block, slot):
        def issue(k, carry):
            for j in range(DMA_UNROLL):
                r = k * DMA_UNROLL + j
                pltpu.make_async_copy(ys_ref.at[pos_ref[block * rows + r]], buf_ref.at[slot, r],
                                      sem.at[slot]).start()
            return carry
        lax.fori_loop(0, rows // DMA_UNROLL, issue, 0)

    @pl.when(i == 0)
    def _():
        fetch(0, 0)

    @pl.when(i + 1 < n_blocks)
    def _():
        fetch(i + 1, (i + 1) % 2)

    slot = i % 2
    pltpu.make_async_copy(ys_ref.at[pl.ds(0, rows)], buf_ref.at[slot], sem.at[slot]).wait()
    y = jnp.concatenate([buf_ref[slot, :, s, :] for s in range(FEATURE_ROWS)], axis=-1)
    out_ref[...] = _layer_norm(alpha * x_ref[...] + y, lng_ref[...], lnb_ref[...])


def _combine(x2, ys, pos, lw, alpha):
    n, D = x2.shape
    rows = min(PROMPT_BLOCK_ROWS, n)
    row_map = lambda i, pos: (i, 0)
    return pl.pallas_call(
        functools.partial(_combine_kernel, alpha=alpha),
        grid_spec=pltpu.PrefetchScalarGridSpec(
            num_scalar_prefetch=1, grid=(n // rows,),
            in_specs=[pl.BlockSpec((rows, D), row_map), pl.BlockSpec(memory_space=pl.ANY),
                      pl.BlockSpec((1, D), lambda i, pos: (0, 0)), pl.BlockSpec((1, D), lambda i, pos: (0, 0))],
            out_specs=pl.BlockSpec((rows, D), row_map),
            scratch_shapes=[pltpu.VMEM((2, rows, FEATURE_ROWS, LANES), F32), pltpu.SemaphoreType.DMA((2,))]),
        out_shape=jax.ShapeDtypeStruct((n, D), F32),
        compiler_params=pltpu.CompilerParams(
            dimension_semantics=("arbitrary",), vmem_limit_bytes=VMEM_LIMIT_BYTES),
        name="combine",
    )(pos, x2, ys, lw['ln_g'][2:3], lw['ln_b'][2:3])


def _moe(x2_p, tok_p, rb_p, x2_s, tok_s, rb_s, cnt, experts, layer, lw, alpha):
    n = x2_p.shape[0] + x2_s.shape[0]
    n_tiles = -(-(n + N_BUCKETS * (EXPERT_TILE - 1)) // EXPERT_TILE)
    base, pad_start, pad_len, used_tiles, ea, eb, valid = _route_tables(cnt, n_tiles)
    pos_p = base[rb_p[0]] + rb_p[1]
    pos_s = base[rb_s[0]] + rb_s[1]
    xs = _dispatch(tok_p, tok_s, pos_p, pos_s, pad_start, pad_len, used_tiles, n_tiles * EXPERT_TILE)
    ys = _experts(xs, ea + layer * N_EXPERTS, eb + layer * N_EXPERTS, valid, *experts)
    return _combine(x2_p, ys, pos_p, lw, alpha), _combine(x2_s, ys, pos_s, lw, alpha)


def _memkv_kernel(mem_ref, wk_ref, wv_ref, k_ref, v_ref, kh_ref, vh_ref):
    mh = _mx(mem_ref[...])
    k = jnp.dot(mh, wk_ref[...], preferred_element_type=F32)
    v = jnp.dot(mh, wv_ref[...], preferred_element_type=F32)
    k_ref[...] = k
    v_ref[...] = v
    kh_ref[...] = k.astype(kh_ref.dtype)
    vh_ref[...] = v.astype(vh_ref.dtype)


def _mem_kv(mem, lw):
    n, D = mem.shape
    rows = min(PROMPT_BLOCK_ROWS, n)
    spec = pl.BlockSpec((rows, D), lambda i: (i, 0))
    return pl.pallas_call(
        _memkv_kernel,
        grid=(n // rows,),
        in_specs=[spec, _const_spec((D, D)), _const_spec((D, D))],
        out_specs=[spec, spec, spec, spec],
        out_shape=[jax.ShapeDtypeStruct((n, D), F32), jax.ShapeDtypeStruct((n, D), F32),
                   jax.ShapeDtypeStruct((n, D), _MXU_DTYPE), jax.ShapeDtypeStruct((n, D), _MXU_DTYPE)],
        compiler_params=pltpu.CompilerParams(
            dimension_semantics=("parallel",), vmem_limit_bytes=VMEM_LIMIT_BYTES),
        name="mem_kv",
    )(mem, lw['w_xk'], lw['w_xv'])


def _layer_weights(l, w_in, b_if, conv_w, gn_m, gn_r, w_out, w_xq, w_xk, w_xv, w_xo, ln_g, ln_b):
    wi = w_in[l]
    gates = wi[:, GATE_COL0:GATE_COL0 + N_GATES]
    b = b_if[l].astype(F32)
    return {
        'w_main': _mx(jnp.concatenate([wi[:, :GATE_COL0], wi[:, GATE_COL0 + N_GATES:]], axis=1)),
        'w_gc': _mx(jnp.pad(gates, ((0, 0), (0, LANES - N_GATES)))),
        'w_gr': _mx(gates.T),
        'b_col': jnp.pad(b, (0, LANES - N_GATES))[None, :],
        'b_row': b[:, None],
        'conv_w': conv_w[l].astype(F32),
        'gn': jnp.concatenate([gn_m[l], gn_r[l]]).astype(F32)[None, :],
        'w_out': _mx(w_out[l]), 'w_xq': _mx(w_xq[l]), 'w_xk': _mx(w_xk[l]), 'w_xv': _mx(w_xv[l]),
        'w_xo': _mx(w_xo[l]),
        'ln_g': ln_g[l].astype(F32), 'ln_b': ln_b[l].astype(F32),
    }


def kernel(x_prompt, x_sample, mem_prompt, cache_mem_k, cache_mem_v, state_conv, state_mlstm_C,
           state_mlstm_n, state_mlstm_m, state_ret_S, w_in, b_if, conv_w, gn_m, gn_r, w_out, w_xq,
           w_xk, w_xv, w_xo, w_router, w_e_gate, w_e_up, w_e_down, ln_g, ln_b):
    depth = w_in.shape[0]
    alpha = (2 * depth) ** 0.25
    Bp, Tp, D = x_prompt.shape
    Bs, Ts, _ = x_sample.shape
    pos_p = jnp.arange(Tp)
    pos_s = PAST_LEN + jnp.arange(Ts)
    w_router_t = w_router.astype(F32).T
    mem2d = mem_prompt.reshape(Bp * N_MEM, D)
    xp, xs = x_prompt, x_sample
    experts = tuple(w.reshape(depth * N_EXPERTS, *w.shape[2:]) for w in (w_e_gate, w_e_up, w_e_down))
    cnt0 = jnp.zeros((BUCKET_ROWS, LANES), F32)
    outs = [[] for _ in range(12)]
    for l in range(depth):
        lw = _layer_weights(l, w_in, b_if, conv_w, gn_m, gn_r, w_out, w_xq, w_xk, w_xv, w_xo, ln_g, ln_b)
        kp, vp, kph, vph = _mem_kv(mem2d, lw)
        x1, cb, c_n, n_n, m_n, s_n = _mixer(xp, lw, pos_p, None, alpha)
        x2_p, tok_p, rb_p, cnt = _xattn(x1, kph.reshape(Bp, N_MEM, D), vph.reshape(Bp, N_MEM, D), lw,
                                        w_router_t, cnt0, alpha)
        for lst, val in zip(outs[:7], (kp.reshape(Bp, N_MEM, X_HEADS, X_DIM),
                                       vp.reshape(Bp, N_MEM, X_HEADS, X_DIM), cb, c_n, n_n, m_n, s_n)):
            lst.append(val)
        state = (state_conv[l], state_mlstm_C[l], state_mlstm_n[l],
                 jnp.broadcast_to(state_mlstm_m[l][:, :, None], (Bs, M_HEADS, LANES)), state_ret_S[l])
        x1, cb, c_n, n_n, m_n, s_n = _mixer(xs, lw, pos_s, state, alpha)
        x2_s, tok_s, rb_s, cnt = _xattn(x1, _mx(cache_mem_k[l].reshape(Bs, N_MEM, D)),
                                        _mx(cache_mem_v[l].reshape(Bs, N_MEM, D)), lw, w_router_t, cnt, alpha)
        for lst, val in zip(outs[7:], (cb, c_n, n_n, m_n, s_n)):
            lst.append(val)
        xp2, xs2 = _moe(x2_p, tok_p, rb_p, x2_s, tok_s, rb_s, cnt, experts, l, lw, alpha)
        xp, xs = xp2.reshape(Bp, Tp, D), xs2.reshape(Bs, Ts, D)
    return (xp, xs) + tuple(jnp.stack(o) for o in outs)
```

```python
import functools

import jax
import jax.numpy as jnp
from jax import lax
from jax.experimental import pallas as pl
from jax.experimental.pallas import tpu as pltpu

F32 = jnp.float32
_MXU_DTYPE = jnp.bfloat16

D_MODEL = 1024
PAST_LEN = 4096
N_MEM = 256
M_WIDTH = D_MODEL // 2
M_HEADS = 4
M_DIM = M_WIDTH // M_HEADS
R_WIDTH = D_MODEL - M_WIDTH
R_HEADS = 4
R_DIM = R_WIDTH // R_HEADS
CONV_W = 4
ROPE_BASE = 10000.0
X_HEADS = 4
X_DIM = D_MODEL // X_HEADS
N_EXPERTS = 16
N_GROUPS = 4
EXP_PER_GROUP = N_EXPERTS // N_GROUPS
D_EXPERT = D_MODEL // 2
LN_EPS = 1e-5
GATE_COL0 = 2 * M_WIDTH + 2 * M_WIDTH
N_GATES = 2 * M_HEADS
MAIN_WIDTH = 4 * D_MODEL

LANES = 128
SUBLANES = 8
VMEM_LIMIT_BYTES = 56 * 1024 * 1024

PROMPT_BLOCK_ROWS = 512
PROMPT_CHUNK = 256
SAMPLE_SEQS_PER_BLOCK = 8

PAIRS = ((0, 1), (0, 2), (0, 3), (1, 2), (1, 3), (2, 3))
N_BUCKETS = N_GROUPS * len(PAIRS)
BUCKET_ROWS = 32
FEATURE_ROWS = D_MODEL // LANES
TOKEN_TILE_ROWS = 2 * SUBLANES
EXPERT_TILE = 256
DMA_UNROLL = 8
ZERO_ROWS = EXPERT_TILE // 2


def _mx(a):
    return a.astype(_MXU_DTYPE)


def _dot(a, b):
    return jnp.dot(_mx(a), _mx(b), preferred_element_type=F32)


def _dot_nt(a, b):
    return lax.dot_general(_mx(a), _mx(b), (((1,), (1,)), ((), ())), preferred_element_type=F32)


def _dot_tn(a, b):
    return lax.dot_general(_mx(a), _mx(b), (((0,), (0,)), ((), ())), preferred_element_type=F32)


def _dot_exact(a, b):
    return jnp.dot(a, b, preferred_element_type=F32, precision=lax.Precision.HIGHEST)


def _layer_norm(y, g, b):
    mu = jnp.mean(y, axis=-1, keepdims=True)
    yc = y - mu
    var = jnp.mean(yc * yc, axis=-1, keepdims=True)
    return yc * lax.rsqrt(var + LN_EPS) * g + b


def _head_norm(h):
    mu = jnp.mean(h, axis=-1, keepdims=True)
    hc = h - mu
    var = jnp.mean(hc * hc, axis=-1, keepdims=True)
    return hc * lax.rsqrt(var + LN_EPS)


def _sigmoid(x):
    return 1.0 / (1.0 + jnp.exp(-x))


def _log_sigmoid(x):
    return jnp.minimum(x, 0.0) - jnp.log1p(jnp.exp(-jnp.abs(x)))


def _const_spec(shape):
    n = len(shape)
    return pl.BlockSpec(shape, lambda *_: (0,) * n)


def _mixer_kernel(*refs, chunk, n_chunks, carried, alpha):
    (x_ref, wmain_ref, wgc_ref, wgr_ref, bcol_ref, brow_ref, convw_ref, gn_ref, wout_ref,
     lng_ref, lnb_ref, cos_ref, sin_ref, dmask_ref, cd_ref, wkr_ref, gl_ref) = refs[:17]
    if carried:
        state_in = None
        rest = refs[17:]
    else:
        state_in = refs[17:22]
        rest = refs[22:]
    x1_ref, conv_out, c_out, n_out, m_out, s_out = rest[:6]
    proj_ref, gcol_ref, hbuf_ref, ub_ref, xh_ref = rest[6:]
    if carried:
        conv_in = None
        c_src, n_src, m_src, s_src = c_out, n_out, m_out, s_out
    else:
        conv_in, c_src, n_src, m_src, s_src = state_in
    L = chunk

    xb = x_ref[...]
    xh = _mx(xb)
    xh_ref[...] = xh
    proj_ref[...] = jnp.dot(xh, wmain_ref[...], preferred_element_type=F32)
    gcol_ref[...] = jnp.dot(xh, wgc_ref[...], preferred_element_type=F32) + bcol_ref[...]

    if carried:
        @pl.when(pl.program_id(1) == 0)
        def _():
            ub_ref[0:SUBLANES, :] = jnp.zeros((SUBLANES, D_MODEL), F32)
            c_out[...] = jnp.zeros(c_out.shape, F32)
            n_out[...] = jnp.zeros(n_out.shape, F32)
            m_out[...] = jnp.zeros(m_out.shape, F32)
            s_out[...] = jnp.zeros(s_out.shape, F32)

    row_i = lax.broadcasted_iota(jnp.int32, (L, L), 0)
    col_i = lax.broadcasted_iota(jnp.int32, (L, L), 1)
    causal = row_i >= col_i
    tri = causal.astype(F32)
    tri_t = (row_i <= col_i).astype(F32)
    convw = convw_ref[...]
    gn = gn_ref[...]

    def chunk_body(c, carry):
        r0 = pl.multiple_of(c * L, L)
        rows = pl.ds(r0, L)
        si = 0 if carried else c
        tab_rows = rows if carried else pl.ds(0, L)

        u = proj_ref[rows, 0:2 * M_WIDTH]
        if not carried:
            ub_ref[SUBLANES - 3:SUBLANES, :] = conv_in[si]
        ub_ref[SUBLANES:SUBLANES + L, :] = u
        acc = u * convw[CONV_W - 1:CONV_W, :]
        for j in range(CONV_W - 1):
            acc = acc + ub_ref[SUBLANES - 3 + j:SUBLANES - 3 + j + L, :] * convw[j:j + 1, :]
        last_rows = ub_ref[SUBLANES + L - 3:SUBLANES + L, :]
        conv_out[si] = last_rows
        if carried:
            ub_ref[SUBLANES - 3:SUBLANES, :] = last_rows
        qk = acc * _sigmoid(acc)

        gc = gcol_ref[rows, :]
        gr = _dot_nt(wgr_ref[...], xh_ref[rows, :]) + brow_ref[...]
        b_cols = _dot_exact(tri, _log_sigmoid(gc))
        b_rows = _dot_exact(_log_sigmoid(gr), tri_t)

        cos = cos_ref[tab_rows, :]
        sin = sin_ref[tab_rows, :]

        for h in range(M_HEADS):
            lo, hi = h * M_DIM, (h + 1) * M_DIM
            q = qk[:, lo:hi] * (M_DIM ** -0.5)
            k = qk[:, M_WIDTH + lo:M_WIDTH + hi]
            v = proj_ref[rows, 2 * M_WIDTH + lo:2 * M_WIDTH + hi]
            og = proj_ref[rows, 3 * M_WIDTH + lo:3 * M_WIDTH + hi]
            c0 = c_src[si, h]
            n0 = n_src[si, h:h + 1, :]
            m0 = m_src[si, h:h + 1, 0:1]
            b_c = b_cols[:, M_HEADS + h:M_HEADS + h + 1]
            b_r = b_rows[M_HEADS + h:M_HEADS + h + 1, :]
            ig_c = gc[:, h:h + 1]
            ig_r = gr[h:h + 1, :]
            a_c = b_c + m0
            dmat = jnp.where(causal, b_c - b_r + ig_r, -jnp.inf)
            m_c = jnp.maximum(a_c, jnp.max(dmat, axis=-1, keepdims=True))
            w_intra = jnp.exp(dmat - m_c)
            w_inter = jnp.exp(a_c - m_c)
            s = _dot_nt(q, k) * w_intra
            num = _dot(s, v) + w_inter * _dot(q, c0)
            den = jnp.sum(s, axis=-1, keepdims=True) + w_inter * jnp.sum(q * n0, axis=-1, keepdims=True)
            hm = num / jnp.maximum(jnp.abs(den), jnp.exp(-m_c))
            m_last = m_c[L - 1:L, :]
            wk_c = jnp.exp(b_c[L - 1:L, :] - b_c + ig_c - m_last)
            decay = jnp.exp(a_c[L - 1:L, :] - m_last)
            kw = wk_c * k
            c_out[si, h] = decay * c0 + _dot_tn(kw, v)
            n_out[si, h:h + 1, :] = decay * n0 + jnp.sum(kw, axis=0, keepdims=True)
            m_out[si, h:h + 1, :] = jnp.broadcast_to(m_last, (1, LANES))
            hm = _head_norm(hm) * gn[:, lo:hi] * _sigmoid(og)
            hbuf_ref[rows, lo:hi] = hm.astype(hbuf_ref.dtype)

        for h in range(R_HEADS):
            lo, hi = h * R_DIM, (h + 1) * R_DIM
            base = 4 * M_WIDTH
            rq = proj_ref[rows, base + lo:base + hi]
            rk = proj_ref[rows, base + R_WIDTH + lo:base + R_WIDTH + hi]
            v = proj_ref[rows, base + 2 * R_WIDTH + lo:base + 2 * R_WIDTH + hi]
            rg = proj_ref[rows, base + 3 * R_WIDTH + lo:base + 3 * R_WIDTH + hi]
            q = rq * cos + pltpu.roll(rq, R_DIM // 2, 1) * sin
            k = (rk * cos + pltpu.roll(rk, R_DIM // 2, 1) * sin) * (R_DIM ** -0.5)
            s0 = s_src[si, h]
            s = _dot_nt(q, k) * dmask_ref[h]
            hr = _dot(s, v) + _dot(q, s0) * cd_ref[h]
            s_out[si, h] = gl_ref[h] * s0 + _dot_tn(wkr_ref[h] * k, v)
            hr = _head_norm(hr) * gn[:, M_WIDTH + lo:M_WIDTH + hi] * (rg * _sigmoid(rg))
            hbuf_ref[rows, M_WIDTH + lo:M_WIDTH + hi] = hr.astype(hbuf_ref.dtype)
        return carry

    lax.fori_loop(0, n_chunks, chunk_body, 0)

    mix = jnp.dot(hbuf_ref[...], wout_ref[...], preferred_element_type=F32)
    x1_ref[...] = _layer_norm(alpha * xb + mix, lng_ref[...], lnb_ref[...])


def _retention_tables(L):
    log_g = jnp.log(1.0 - 2.0 ** (-5.0 - jnp.arange(R_HEADS, dtype=F32)))
    idx = jnp.arange(L, dtype=F32)
    rel = idx[:, None] - idx[None, :]
    dmask = jnp.where(rel >= 0, jnp.exp(log_g[:, None, None] * jnp.maximum(rel, 0.0)), 0.0)
    cd = jnp.exp(log_g[:, None] * (idx + 1.0))
    wk = jnp.exp(log_g[:, None] * (L - 1.0 - idx))
    gl = jnp.exp(log_g * L)
    bc = lambda t: jnp.broadcast_to(t[:, :, None], (R_HEADS, L, LANES))
    return dmask, bc(cd), bc(wk), jnp.broadcast_to(gl[:, None, None], (R_HEADS, 1, LANES))


def _rotary_tables(pos):
    half = R_DIM // 2
    inv = ROPE_BASE ** (-jnp.arange(half, dtype=F32) / half)
    ang = pos.astype(F32)[:, None] * inv[None, :]
    cos, sin = jnp.cos(ang), jnp.sin(ang)
    return jnp.concatenate([cos, cos], axis=-1), jnp.concatenate([-sin, sin], axis=-1)


def _mixer(x, lw, pos, state, alpha):
    S, T, D = x.shape
    carried = state is None
    if carried:
        block_rows = min(PROMPT_BLOCK_ROWS, T)
        L = min(PROMPT_CHUNK, block_rows)
        seqs = 1
        n_t = T // block_rows
        grid = (S, n_t)
    else:
        seqs = min(SAMPLE_SEQS_PER_BLOCK, S)
        L = T
        block_rows = seqs * T
        n_t = 1
        grid = (S // seqs, 1)
    n_chunks = block_rows // L
    xr = x.reshape(S * T, D)
    cos, sin = _rotary_tables(pos)
    dmask, cd, wkr, gl = _retention_tables(L)

    row_spec = pl.BlockSpec((block_rows, D), lambda b, t: (b * n_t + t, 0))
    tab_spec = pl.BlockSpec((block_rows if carried else T, LANES), lambda b, t: (t, 0))
    in_specs = [
        row_spec,
        _const_spec((D, MAIN_WIDTH)), _const_spec((D, LANES)), _const_spec((N_GATES, D)),
        _const_spec((1, LANES)), _const_spec((N_GATES, 1)), _const_spec((CONV_W, 2 * M_WIDTH)),
        _const_spec((1, D)), _const_spec((D, D)), _const_spec((1, D)), _const_spec((1, D)),
        tab_spec, tab_spec,
        _const_spec((R_HEADS, L, L)), _const_spec((R_HEADS, L, LANES)), _const_spec((R_HEADS, L, LANES)),
        _const_spec((R_HEADS, 1, LANES)),
    ]
    args = [xr, lw['w_main'], lw['w_gc'], lw['w_gr'], lw['b_col'], lw['b_row'], lw['conv_w'],
            lw['gn'], lw['w_out'], lw['ln_g'][0:1], lw['ln_b'][0:1], cos, sin, dmask, cd, wkr, gl]
    state_specs = [
        pl.BlockSpec((seqs, CONV_W - 1, 2 * M_WIDTH), lambda b, t: (b, 0, 0)),
        pl.BlockSpec((seqs, M_HEADS, M_DIM, M_DIM), lambda b, t: (b, 0, 0, 0)),
        pl.BlockSpec((seqs, M_HEADS, M_DIM), lambda b, t: (b, 0, 0)),
        pl.BlockSpec((seqs, M_HEADS, LANES), lambda b, t: (b, 0, 0)),
        pl.BlockSpec((seqs, R_HEADS, R_DIM, R_DIM), lambda b, t: (b, 0, 0, 0)),
    ]
    if not carried:
        in_specs += state_specs
        args += list(state)
    out_shape = [
        jax.ShapeDtypeStruct((S * T, D), F32),
        jax.ShapeDtypeStruct((S, CONV_W - 1, 2 * M_WIDTH), F32),
        jax.ShapeDtypeStruct((S, M_HEADS, M_DIM, M_DIM), F32),
        jax.ShapeDtypeStruct((S, M_HEADS, M_DIM), F32),
        jax.ShapeDtypeStruct((S, M_HEADS, LANES), F32),
        jax.ShapeDtypeStruct((S, R_HEADS, R_DIM, R_DIM), F32),
    ]
    out_specs = [row_spec] + state_specs
    scratch = [
        pltpu.VMEM((block_rows, MAIN_WIDTH), F32),
        pltpu.VMEM((block_rows, LANES), F32),
        pltpu.VMEM((block_rows, D), _MXU_DTYPE),
        pltpu.VMEM((SUBLANES + L, 2 * M_WIDTH), F32),
        pltpu.VMEM((block_rows, D), _MXU_DTYPE),
    ]
    outs = pl.pallas_call(
        functools.partial(_mixer_kernel, chunk=L, n_chunks=n_chunks, carried=carried, alpha=alpha),
        grid=grid, in_specs=in_specs, out_specs=out_specs, out_shape=out_shape,
        scratch_shapes=scratch,
        compiler_params=pltpu.CompilerParams(
            dimension_semantics=("parallel", "arbitrary"), vmem_limit_bytes=VMEM_LIMIT_BYTES),
        name="mixer_prompt" if carried else "mixer_sample",
    )(*args)
    x1, conv_new, c_new, n_new, m_new, s_new = outs
    return x1.reshape(S, T, D), conv_new, c_new, n_new, m_new[:, :, 0], s_new


def _route_rows(logits_t):
    mx = jnp.max(logits_t, axis=0, keepdims=True)
    ex = jnp.exp(logits_t - mx)
    probs = ex / jnp.sum(ex, axis=0, keepdims=True)
    p = [probs[e:e + 1, :] for e in range(N_EXPERTS)]

    def first_max(vals):
        m = vals[0]
        for t in vals[1:]:
            m = jnp.maximum(m, t)
        taken = None
        sel = []
        for t in vals:
            is_max = t == m
            if taken is None:
                pick, taken = is_max, is_max
            else:
                pick = jnp.logical_and(is_max, jnp.logical_not(taken))
                taken = jnp.logical_or(taken, is_max)
            sel.append(pick)
        return m, sel

    scores, picked, denom = [], [], []
    for g in range(N_GROUPS):
        vals = p[g * EXP_PER_GROUP:(g + 1) * EXP_PER_GROUP]
        m1, sel1 = first_max(vals)
        rest = [jnp.where(s1, -1.0, t) for s1, t in zip(sel1, vals)]
        m2, sel2 = first_max(rest)
        scores.append(m1 + m2)
        denom.append(m1 + m2)
        picked.append([jnp.logical_or(a, b) for a, b in zip(sel1, sel2)])
    _, gsel = first_max(scores)
    bucket = jnp.zeros(p[0].shape, jnp.int32)
    ca = jnp.zeros(p[0].shape, F32)
    cb = jnp.zeros(p[0].shape, F32)
    for g in range(N_GROUPS):
        w = [p[g * EXP_PER_GROUP + j] / denom[g] for j in range(EXP_PER_GROUP)]
        for pi, (a, b) in enumerate(PAIRS):
            hit = jnp.logical_and(gsel[g], jnp.logical_and(picked[g][a], picked[g][b]))
            bucket = jnp.where(hit, g * len(PAIRS) + pi, bucket)
            ca = jnp.where(hit, w[a], ca)
            cb = jnp.where(hit, w[b], cb)
    return bucket, ca, cb


def _xattn_kernel(x_ref, k_ref, v_ref, wq_ref, wo_ref, lng_ref, lnb_ref, wr_ref, upper_ref, cnt_in_ref,
                  x2_ref, tok_ref, rb_ref, cnt_out_ref, q_ref, o_ref, run_ref, *, seqs, seq_rows, alpha):
    @pl.when(jnp.logical_and(pl.program_id(0) == 0, pl.program_id(1) == 0))
    def _():
        run_ref[...] = cnt_in_ref[...]

    xb = x_ref[...]
    q_ref[...] = _dot(xb, wq_ref[...]).astype(q_ref.dtype)
    scale = X_DIM ** -0.5
    for sq in range(seqs):
        r0 = sq * seq_rows
        for h in range(X_HEADS):
            lo, hi = h * X_DIM, (h + 1) * X_DIM
            s = _dot_nt(q_ref[r0:r0 + seq_rows, lo:hi], k_ref[sq, :, lo:hi]) * scale
            e = jnp.exp(s - jnp.max(s, axis=-1, keepdims=True))
            p = e / jnp.sum(e, axis=-1, keepdims=True)
            o_ref[r0:r0 + seq_rows, lo:hi] = _dot(p, v_ref[sq, :, lo:hi]).astype(o_ref.dtype)
    att = jnp.dot(o_ref[...], wo_ref[...], preferred_element_type=F32)
    x2 = _layer_norm(alpha * xb + att, lng_ref[...], lnb_ref[...])
    x2_ref[...] = x2
    n = xb.shape[0]

    logits_t = lax.dot_general(wr_ref[...], x2, (((1,), (1,)), ((), ())),
                               preferred_element_type=F32, precision=lax.Precision.HIGHEST)
    bucket, ca, cb = _route_rows(logits_t)

    onehot = (lax.broadcasted_iota(jnp.int32, (BUCKET_ROWS, n), 0) == bucket).astype(F32)
    earlier = jnp.dot(_mx(onehot), upper_ref[...], preferred_element_type=F32)
    run = run_ref[...]
    rank = jnp.sum(onehot * (earlier + run[:, 0:1]), axis=0, keepdims=True)
    run_ref[...] = run + jnp.sum(onehot, axis=1, keepdims=True)
    cnt_out_ref[...] = run_ref[...]
    rb_ref[...] = jnp.concatenate(
        [bucket, rank.astype(jnp.int32), jnp.zeros((SUBLANES - 2, n), jnp.int32)], axis=0)

    def tile_row(s):
        return pl.ds(s, n, stride=TOKEN_TILE_ROWS)
    for s in range(FEATURE_ROWS):
        tok_ref[tile_row(s), :] = x2[:, s * LANES:(s + 1) * LANES]
    cw_t = jnp.concatenate([ca, cb, jnp.zeros((LANES - 2, n), F32)], axis=0)
    tok_ref[tile_row(FEATURE_ROWS), :] = cw_t.T
    for s in range(FEATURE_ROWS + 1, TOKEN_TILE_ROWS):
        tok_ref[tile_row(s), :] = jnp.zeros((n, LANES), F32)


def _xattn(x, mem_k, mem_v, lw, w_router_t, cnt_in, alpha):
    S, T, D = x.shape
    if T >= PROMPT_BLOCK_ROWS:
        seqs, seq_rows = 1, PROMPT_BLOCK_ROWS
    else:
        seqs, seq_rows = min(SAMPLE_SEQS_PER_BLOCK, S), T
    n_t = T // seq_rows
    block_rows = seqs * seq_rows
    grid = (S // seqs, n_t)
    row_map = lambda b, t: (b * n_t + t, 0)
    kv_spec = pl.BlockSpec((seqs, N_MEM, D), lambda b, t: (b, 0, 0))
    idx = jnp.arange(block_rows)
    upper = _mx(idx[:, None] < idx[None, :])
    return pl.pallas_call(
        functools.partial(_xattn_kernel, seqs=seqs, seq_rows=seq_rows, alpha=alpha),
        grid=grid,
        in_specs=[pl.BlockSpec((block_rows, D), row_map), kv_spec, kv_spec,
                  _const_spec((D, D)), _const_spec((D, D)), _const_spec((1, D)), _const_spec((1, D)),
                  _const_spec((N_EXPERTS, D)), _const_spec((block_rows, block_rows)),
                  _const_spec((BUCKET_ROWS, LANES))],
        out_specs=[pl.BlockSpec((block_rows, D), row_map),
                   pl.BlockSpec((block_rows * TOKEN_TILE_ROWS, LANES), row_map),
                   pl.BlockSpec((SUBLANES, block_rows), lambda b, t: (0, b * n_t + t)),
                   _const_spec((BUCKET_ROWS, LANES))],
        out_shape=[jax.ShapeDtypeStruct((S * T, D), F32),
                   jax.ShapeDtypeStruct((S * T * TOKEN_TILE_ROWS, LANES), F32),
                   jax.ShapeDtypeStruct((SUBLANES, S * T), jnp.int32),
                   jax.ShapeDtypeStruct((BUCKET_ROWS, LANES), F32)],
        scratch_shapes=[pltpu.VMEM((block_rows, D), _MXU_DTYPE), pltpu.VMEM((block_rows, D), _MXU_DTYPE),
                        pltpu.VMEM((BUCKET_ROWS, LANES), F32)],
        compiler_params=pltpu.CompilerParams(
            dimension_semantics=("arbitrary", "arbitrary"), vmem_limit_bytes=VMEM_LIMIT_BYTES),
        name="xattn",
    )(x.reshape(S * T, D), mem_k, mem_v, lw['w_xq'], lw['w_xo'], lw['ln_g'][1:2], lw['ln_b'][1:2],
      w_router_t, upper, cnt_in)


def _route_tables(cnt, n_tiles):
    cnt = cnt[:N_BUCKETS, 0].astype(jnp.int32)
    padded = ((cnt + EXPERT_TILE - 1) // EXPERT_TILE) * EXPERT_TILE
    ends = jnp.cumsum(padded)
    base = ends - padded
    tile_start = jnp.arange(n_tiles, dtype=jnp.int32) * EXPERT_TILE
    valid = tile_start < ends[-1]
    last_start = jnp.maximum(ends[-1] - EXPERT_TILE, 0)
    start = jnp.where(valid, tile_start, last_start)
    tb = jnp.sum((ends[None, :] <= start[:, None]).astype(jnp.int32), axis=1)
    tb = jnp.minimum(tb, N_BUCKETS - 1)
    group, pair = tb // len(PAIRS), tb % len(PAIRS)
    pair_a = jnp.asarray([a for a, _ in PAIRS], jnp.int32)
    pair_b = jnp.asarray([b for _, b in PAIRS], jnp.int32)
    ea = group * EXP_PER_GROUP + pair_a[pair]
    eb = group * EXP_PER_GROUP + pair_b[pair]
    used_tiles = (ends[-1:] // EXPERT_TILE).astype(jnp.int32)
    return base, base + cnt, padded - cnt, used_tiles, ea, eb, valid.astype(jnp.int32)


def _dispatch_kernel(pos_ref, pad_start_ref, pad_len_ref, used_tiles_ref, tok_p_ref, tok_s_ref,
                     xs_ref, zero_ref, sem, pad_sem, *, blocks_p):
    i = pl.program_id(0)
    rows = tok_p_ref.shape[0]
    n_tiles = xs_ref.shape[0] // EXPERT_TILE

    def unused_tile(t, wait):
        cp = pltpu.make_async_copy(zero_ref, xs_ref.at[pl.ds(t * EXPERT_TILE, EXPERT_TILE)], pad_sem)
        cp.wait() if wait else cp.start()

    def pad_copies(b, wait):
        off = pad_start_ref[b]
        left = pad_len_ref[b]
        size = ZERO_ROWS
        while size >= 1:
            cp = pltpu.make_async_copy(zero_ref.at[pl.ds(0, size)], xs_ref.at[pl.ds(off, size)], pad_sem)

            @pl.when(left >= size)
            def _():
                cp.wait() if wait else cp.start()
            take = left >= size
            off = jnp.where(take, off + size, off)
            left = jnp.where(take, left - size, left)
            size //= 2

    @pl.when(i == 0)
    def _():
        zero_ref[...] = jnp.zeros(zero_ref.shape, F32)
        for wait in (False, True):
            lax.fori_loop(used_tiles_ref[0], n_tiles, lambda t, c, w=wait: (unused_tile(t, w), c)[1], 0)
            lax.fori_loop(0, N_BUCKETS, lambda b, c, w=wait: (pad_copies(b, w), c)[1], 0)

    def scatter(tok_ref):
        def issue(k, carry):
            for j in range(DMA_UNROLL):
                r = k * DMA_UNROLL + j
                pltpu.make_async_copy(tok_ref.at[r], xs_ref.at[pos_ref[i * rows + r]], sem).start()
            return carry
        lax.fori_loop(0, rows // DMA_UNROLL, issue, 0)
        pltpu.make_async_copy(tok_ref, xs_ref.at[pl.ds(0, rows)], sem).wait()

    @pl.when(i < blocks_p)
    def _():
        scatter(tok_p_ref)

    @pl.when(i >= blocks_p)
    def _():
        scatter(tok_s_ref)


def _dispatch(tok_p, tok_s, pos, pad_start, pad_len, used_tiles, n_rows):
    n_p, n_s = tok_p.shape[0], tok_s.shape[0]
    rows = min(PROMPT_BLOCK_ROWS, n_p, n_s)
    blocks_p, blocks_s = n_p // rows, n_s // rows
    block = (rows, TOKEN_TILE_ROWS, LANES)
    return pl.pallas_call(
        functools.partial(_dispatch_kernel, blocks_p=blocks_p),
        grid_spec=pltpu.PrefetchScalarGridSpec(
            num_scalar_prefetch=4, grid=(blocks_p + blocks_s,),
            in_specs=[pl.BlockSpec(block, lambda i, *_: (jnp.minimum(i, blocks_p - 1), 0, 0)),
                      pl.BlockSpec(block, lambda i, *_: (jnp.maximum(i - blocks_p, 0), 0, 0))],
            out_specs=pl.BlockSpec(memory_space=pl.ANY),
            scratch_shapes=[pltpu.VMEM((EXPERT_TILE, TOKEN_TILE_ROWS, LANES), F32),
                            pltpu.SemaphoreType.DMA, pltpu.SemaphoreType.DMA]),
        out_shape=jax.ShapeDtypeStruct((n_rows, TOKEN_TILE_ROWS, LANES), F32),
        compiler_params=pltpu.CompilerParams(
            dimension_semantics=("arbitrary",), vmem_limit_bytes=VMEM_LIMIT_BYTES),
        name="dispatch",
    )(pos, pad_start, pad_len, used_tiles, tok_p, tok_s)


def _expert_kernel(ea_ref, eb_ref, valid_ref, xs_ref, wga_ref, wua_ref, wda_ref, wgb_ref, wub_ref, wdb_ref,
                   ys_ref, ga_ref, ua_ref, da_ref, gb_ref, ub_ref, db_ref):
    t = pl.program_id(0)

    @pl.when(valid_ref[t] > 0)
    def _():
        prev = jnp.maximum(t - 1, 0)

        @pl.when(jnp.logical_or(t == 0, ea_ref[t] != ea_ref[prev]))
        def _():
            ga_ref[...] = _mx(wga_ref[0])
            ua_ref[...] = _mx(wua_ref[0])
            da_ref[...] = _mx(wda_ref[0])

        @pl.when(jnp.logical_or(t == 0, eb_ref[t] != eb_ref[prev]))
        def _():
            gb_ref[...] = _mx(wgb_ref[0])
            ub_ref[...] = _mx(wub_ref[0])
            db_ref[...] = _mx(wdb_ref[0])

        def tile_row(s):
            return xs_ref[pl.ds(s, EXPERT_TILE, stride=TOKEN_TILE_ROWS), :]
        x = _mx(jnp.concatenate([tile_row(s) for s in range(FEATURE_ROWS)], axis=-1))
        cw = tile_row(FEATURE_ROWS)

        def ffn(g_ref, u_ref, d_ref):
            gate = jnp.dot(x, g_ref[...], preferred_element_type=F32)
            up = jnp.dot(x, u_ref[...], preferred_element_type=F32)
            return _dot(gate * _sigmoid(gate) * up, d_ref[...])
        y = cw[:, 0:1] * ffn(ga_ref, ua_ref, da_ref) + cw[:, 1:2] * ffn(gb_ref, ub_ref, db_ref)
        for s in range(FEATURE_ROWS):
            ys_ref[pl.ds(s, EXPERT_TILE, stride=FEATURE_ROWS), :] = y[:, s * LANES:(s + 1) * LANES]

    @pl.when(valid_ref[t] == 0)
    def _():
        ys_ref[...] = jnp.zeros(ys_ref.shape, F32)


def _experts(xs, ea, eb, valid, w_e_gate, w_e_up, w_e_down):
    n_tiles = ea.shape[0]
    D = D_MODEL
    wa = lambda t, ea, eb, v: (ea[t], 0, 0)
    wb = lambda t, ea, eb, v: (eb[t], 0, 0)
    gu = (1, D, D_EXPERT)
    dn = (1, D_EXPERT, D)
    return pl.pallas_call(
        _expert_kernel,
        grid_spec=pltpu.PrefetchScalarGridSpec(
            num_scalar_prefetch=3, grid=(n_tiles,),
            in_specs=[pl.BlockSpec((EXPERT_TILE * TOKEN_TILE_ROWS, LANES), lambda t, ea, eb, v: (t * v[t], 0)),
                      pl.BlockSpec(gu, wa), pl.BlockSpec(gu, wa), pl.BlockSpec(dn, wa),
                      pl.BlockSpec(gu, wb), pl.BlockSpec(gu, wb), pl.BlockSpec(dn, wb)],
            out_specs=pl.BlockSpec((EXPERT_TILE * FEATURE_ROWS, LANES), lambda t, ea, eb, v: (t, 0)),
            scratch_shapes=[pltpu.VMEM((D, D_EXPERT), _MXU_DTYPE), pltpu.VMEM((D, D_EXPERT), _MXU_DTYPE),
                            pltpu.VMEM((D_EXPERT, D), _MXU_DTYPE), pltpu.VMEM((D, D_EXPERT), _MXU_DTYPE),
                            pltpu.VMEM((D, D_EXPERT), _MXU_DTYPE), pltpu.VMEM((D_EXPERT, D), _MXU_DTYPE)]),
        out_shape=jax.ShapeDtypeStruct((n_tiles * EXPERT_TILE * FEATURE_ROWS, LANES), F32),
        compiler_params=pltpu.CompilerParams(
            dimension_semantics=("arbitrary",), vmem_limit_bytes=VMEM_LIMIT_BYTES),
        name="experts",
    )(ea, eb, valid, xs, w_e_gate, w_e_up, w_e_down, w_e_gate, w_e_up, w_e_down)


def _combine_kernel(pos_ref, x_ref, ys_ref, lng_ref, lnb_ref, out_ref, buf0_ref, buf1_ref, sem, *, alpha):
    i = pl.program_id(0)
    n_steps = pl.num_programs(0)
    half = x_ref.shape[0] // 2
    bufs = (buf0_ref, buf1_ref)

    def fetch(block, slot):
        def issue(k, carry):
            for j in range(DMA_UNROLL):
                r = k * DMA_UNROLL + j
                src = pl.multiple_of(pos_ref[block * half + r] * FEATURE_ROWS, FEATURE_ROWS)
                pltpu.make_async_copy(ys_ref.at[pl.ds(src, FEATURE_ROWS)],
                                      bufs[slot].at[pl.ds(r * FEATURE_ROWS, FEATURE_ROWS)], sem.at[slot]).start()
            return carry
        lax.fori_loop(0, half // DMA_UNROLL, issue, 0)

    def finish(slot):
        buf = bufs[slot]
        pltpu.make_async_copy(ys_ref.at[pl.ds(0, half * FEATURE_ROWS)], buf, sem.at[slot]).wait()
        y = jnp.concatenate([buf[pl.ds(s, half, stride=FEATURE_ROWS), :] for s in range(FEATURE_ROWS)], axis=-1)
        rows = pl.ds(slot * half, half)
        out_ref[rows, :] = _layer_norm(alpha * x_ref[rows, :] + y, lng_ref[...], lnb_ref[...])

    @pl.when(i == 0)
    def _():
        fetch(0, 0)

    fetch(2 * i + 1, 1)
    finish(0)

    @pl.when(i + 1 < n_steps)
    def _():
        fetch(2 * i + 2, 0)

    finish(1)


def _combine(x2, ys, pos, lw, alpha):
    n, D = x2.shape
    rows = min(2 * PROMPT_BLOCK_ROWS, n)
    row_map = lambda i, pos: (i, 0)
    half_buf = pltpu.VMEM((rows // 2 * FEATURE_ROWS, LANES), F32)
    return pl.pallas_call(
        functools.partial(_combine_kernel, alpha=alpha),
        grid_spec=pltpu.PrefetchScalarGridSpec(
            num_scalar_prefetch=1, grid=(n // rows,),
            in_specs=[pl.BlockSpec((rows, D), row_map), pl.BlockSpec(memory_space=pl.ANY),
                      pl.BlockSpec((1, D), lambda i, pos: (0, 0)), pl.BlockSpec((1, D), lambda i, pos: (0, 0))],
            out_specs=pl.BlockSpec((rows, D), row_map),
            scratch_shapes=[half_buf, half_buf, pltpu.SemaphoreType.DMA((2,))]),
        out_shape=jax.ShapeDtypeStruct((n, D), F32),
        compiler_params=pltpu.CompilerParams(
            dimension_semantics=("arbitrary",), vmem_limit_bytes=VMEM_LIMIT_BYTES),
        name="combine",
    )(pos, x2, ys, lw['ln_g'][2:3], lw['ln_b'][2:3])


def _moe(x2_p, tok_p, rb_p, x2_s, tok_s, rb_s, cnt, experts, layer, lw, alpha):
    n_p, n_s = x2_p.shape[0], x2_s.shape[0]
    n_tiles = -(-(n_p + n_s + N_BUCKETS * (EXPERT_TILE - 1)) // EXPERT_TILE)
    base, pad_start, pad_len, used_tiles, ea, eb, valid = _route_tables(cnt, n_tiles)
    pos_p = base[rb_p[0]] + rb_p[1]
    pos_s = base[rb_s[0]] + rb_s[1]
    tile3 = lambda t: t.reshape(-1, TOKEN_TILE_ROWS, LANES)
    xs = _dispatch(tile3(tok_p), tile3(tok_s), jnp.concatenate([pos_p, pos_s]), pad_start, pad_len, used_tiles,
                   n_tiles * EXPERT_TILE)
    ys = _experts(xs.reshape(-1, LANES), ea + layer * N_EXPERTS, eb + layer * N_EXPERTS, valid, *experts)
    return _combine(x2_p, ys, pos_p, lw, alpha), _combine(x2_s, ys, pos_s, lw, alpha)


def _memkv_kernel(mem_ref, wk_ref, wv_ref, k_ref, v_ref, kh_ref, vh_ref):
    mh = _mx(mem_ref[...])
    k = jnp.dot(mh, wk_ref[...], preferred_element_type=F32)
    v = jnp.dot(mh, wv_ref[...], preferred_element_type=F32)
    k_ref[...] = k
    v_ref[...] = v
    kh_ref[...] = k.astype(kh_ref.dtype)
    vh_ref[...] = v.astype(vh_ref.dtype)


def _mem_kv(mem, lw):
    n, D = mem.shape
    rows = min(PROMPT_BLOCK_ROWS, n)
    spec = pl.BlockSpec((rows, D), lambda i: (i, 0))
    return pl.pallas_call(
        _memkv_kernel,
        grid=(n // rows,),
        in_specs=[spec, _const_spec((D, D)), _const_spec((D, D))],
        out_specs=[spec, spec, spec, spec],
        out_shape=[jax.ShapeDtypeStruct((n, D), F32), jax.ShapeDtypeStruct((n, D), F32),
                   jax.ShapeDtypeStruct((n, D), _MXU_DTYPE), jax.ShapeDtypeStruct((n, D), _MXU_DTYPE)],
        compiler_params=pltpu.CompilerParams(
            dimension_semantics=("parallel",), vmem_limit_bytes=VMEM_LIMIT_BYTES),
        name="mem_kv",
    )(mem, lw['w_xk'], lw['w_xv'])


def _layer_weights(l, w_in, b_if, conv_w, gn_m, gn_r, w_out, w_xq, w_xk, w_xv, w_xo, ln_g, ln_b):
    wi = w_in[l]
    gates = wi[:, GATE_COL0:GATE_COL0 + N_GATES]
    b = b_if[l].astype(F32)
    return {
        'w_main': _mx(jnp.concatenate([wi[:, :GATE_COL0], wi[:, GATE_COL0 + N_GATES:]], axis=1)),
        'w_gc': _mx(jnp.pad(gates, ((0, 0), (0, LANES - N_GATES)))),
        'w_gr': _mx(gates.T),
        'b_col': jnp.pad(b, (0, LANES - N_GATES))[None, :],
        'b_row': b[:, None],
        'conv_w': conv_w[l].astype(F32),
        'gn': jnp.concatenate([gn_m[l], gn_r[l]]).astype(F32)[None, :],
        'w_out': _mx(w_out[l]), 'w_xq': _mx(w_xq[l]), 'w_xk': _mx(w_xk[l]), 'w_xv': _mx(w_xv[l]),
        'w_xo': _mx(w_xo[l]),
        'ln_g': ln_g[l].astype(F32), 'ln_b': ln_b[l].astype(F32),
    }


def kernel(x_prompt, x_sample, mem_prompt, cache_mem_k, cache_mem_v, state_conv, state_mlstm_C,
           state_mlstm_n, state_mlstm_m, state_ret_S, w_in, b_if, conv_w, gn_m, gn_r, w_out, w_xq,
           w_xk, w_xv, w_xo, w_router, w_e_gate, w_e_up, w_e_down, ln_g, ln_b):
    depth = w_in.shape[0]
    alpha = (2 * depth) ** 0.25
    Bp, Tp, D = x_prompt.shape
    Bs, Ts, _ = x_sample.shape
    pos_p = jnp.arange(Tp)
    pos_s = PAST_LEN + jnp.arange(Ts)
    w_router_t = w_router.astype(F32).T
    mem2d = mem_prompt.reshape(Bp * N_MEM, D)
    xp, xs = x_prompt, x_sample
    experts = tuple(w.reshape(depth * N_EXPERTS, *w.shape[2:]) for w in (w_e_gate, w_e_up, w_e_down))
    cnt0 = jnp.zeros((BUCKET_ROWS, LANES), F32)
    outs = [[] for _ in range(12)]
    for l in range(depth):
        lw = _layer_weights(l, w_in, b_if, conv_w, gn_m, gn_r, w_out, w_xq, w_xk, w_xv, w_xo, ln_g, ln_b)
        kp, vp, kph, vph = _mem_kv(mem2d, lw)
        x1, cb, c_n, n_n, m_n, s_n = _mixer(xp, lw, pos_p, None, alpha)
        x2_p, tok_p, rb_p, cnt = _xattn(x1, kph.reshape(Bp, N_MEM, D), vph.reshape(Bp, N_MEM, D), lw,
                                        w_router_t, cnt0, alpha)
        for lst, val in zip(outs[:7], (kp.reshape(Bp, N_MEM, X_HEADS, X_DIM),
                                       vp.reshape(Bp, N_MEM, X_HEADS, X_DIM), cb, c_n, n_n, m_n, s_n)):
            lst.append(val)
        state = (state_conv[l], state_mlstm_C[l], state_mlstm_n[l],
                 jnp.broadcast_to(state_mlstm_m[l][:, :, None], (Bs, M_HEADS, LANES)), state_ret_S[l])
        x1, cb, c_n, n_n, m_n, s_n = _mixer(xs, lw, pos_s, state, alpha)
        x2_s, tok_s, rb_s, cnt = _xattn(x1, _mx(cache_mem_k[l].reshape(Bs, N_MEM, D)),
                                        _mx(cache_mem_v[l].reshape(Bs, N_MEM, D)), lw, w_router_t, cnt, alpha)
        for lst, val in zip(outs[7:], (cb, c_n, n_n, m_n, s_n)):
            lst.append(val)
        xp2, xs2 = _moe(x2_p, tok_p, rb_p, x2_s, tok_s, rb_s, cnt, experts, l, lw, alpha)
        xp, xs = xp2.reshape(Bp, Tp, D), xs2.reshape(Bs, Ts, D)
    return (xp, xs) + tuple(jnp.stack(o) for o in outs)
```

```python
import functools

import jax
import jax.numpy as jnp
from jax import lax
from jax.experimental import pallas as pl
from jax.experimental.pallas import tpu as pltpu

F32 = jnp.float32
_MXU_DTYPE = jnp.bfloat16

D_MODEL = 1024
PAST_LEN = 4096
N_MEM = 256
M_WIDTH = D_MODEL // 2
M_HEADS = 4
M_DIM = M_WIDTH // M_HEADS
R_WIDTH = D_MODEL - M_WIDTH
R_HEADS = 4
R_DIM = R_WIDTH // R_HEADS
CONV_W = 4
ROPE_BASE = 10000.0
X_HEADS = 4
X_DIM = D_MODEL // X_HEADS
N_EXPERTS = 16
N_GROUPS = 4
EXP_PER_GROUP = N_EXPERTS // N_GROUPS
D_EXPERT = D_MODEL // 2
LN_EPS = 1e-5
GATE_COL0 = 2 * M_WIDTH + 2 * M_WIDTH
N_GATES = 2 * M_HEADS
MAIN_WIDTH = 4 * D_MODEL

LANES = 128
SUBLANES = 8
VMEM_LIMIT_BYTES = 56 * 1024 * 1024

PROMPT_BLOCK_ROWS = 512
PROMPT_CHUNK = 256
SAMPLE_SEQS_PER_BLOCK = 8

PAIRS = ((0, 1), (0, 2), (0, 3), (1, 2), (1, 3), (2, 3))
N_BUCKETS = N_GROUPS * len(PAIRS)
BUCKET_ROWS = 32
EXPERT_TILE = 256
ROW_WIDTH = D_MODEL + LANES
FEATURE_ROWS = D_MODEL // LANES
DMA_UNROLL = 8
ZERO_ROWS = EXPERT_TILE // 2


def _mx(a):
    return a.astype(_MXU_DTYPE)


def _dot(a, b):
    return jnp.dot(_mx(a), _mx(b), preferred_element_type=F32)


def _dot_nt(a, b):
    return lax.dot_general(_mx(a), _mx(b), (((1,), (1,)), ((), ())), preferred_element_type=F32)


def _dot_tn(a, b):
    return lax.dot_general(_mx(a), _mx(b), (((0,), (0,)), ((), ())), preferred_element_type=F32)


def _dot_exact(a, b):
    return jnp.dot(a, b, preferred_element_type=F32, precision=lax.Precision.HIGHEST)


def _layer_norm(y, g, b):
    mu = jnp.mean(y, axis=-1, keepdims=True)
    yc = y - mu
    var = jnp.mean(yc * yc, axis=-1, keepdims=True)
    return yc * lax.rsqrt(var + LN_EPS) * g + b


def _head_norm(h):
    mu = jnp.mean(h, axis=-1, keepdims=True)
    hc = h - mu
    var = jnp.mean(hc * hc, axis=-1, keepdims=True)
    return hc * lax.rsqrt(var + LN_EPS)


def _sigmoid(x):
    return 1.0 / (1.0 + jnp.exp(-x))


def _log_sigmoid(x):
    return jnp.minimum(x, 0.0) - jnp.log1p(jnp.exp(-jnp.abs(x)))


def _const_spec(shape):
    n = len(shape)
    return pl.BlockSpec(shape, lambda *_: (0,) * n)


def _mixer_kernel(*refs, chunk, n_chunks, carried, alpha):
    (x_ref, wmain_ref, wgc_ref, wgr_ref, bcol_ref, brow_ref, convw_ref, gn_ref, wout_ref,
     lng_ref, lnb_ref, cos_ref, sin_ref, dmask_ref, cd_ref, wkr_ref, gl_ref) = refs[:17]
    if carried:
        state_in = None
        rest = refs[17:]
    else:
        state_in = refs[17:22]
        rest = refs[22:]
    x1_ref, conv_out, c_out, n_out, m_out, s_out = rest[:6]
    proj_ref, gcol_ref, hbuf_ref, ub_ref, xh_ref = rest[6:]
    if carried:
        conv_in = None
        c_src, n_src, m_src, s_src = c_out, n_out, m_out, s_out
    else:
        conv_in, c_src, n_src, m_src, s_src = state_in
    L = chunk

    xb = x_ref[...]
    xh = _mx(xb)
    xh_ref[...] = xh
    proj_ref[...] = jnp.dot(xh, wmain_ref[...], preferred_element_type=F32)
    gcol_ref[...] = jnp.dot(xh, wgc_ref[...], preferred_element_type=F32) + bcol_ref[...]

    if carried:
        @pl.when(pl.program_id(1) == 0)
        def _():
            ub_ref[0:SUBLANES, :] = jnp.zeros((SUBLANES, D_MODEL), F32)
            c_out[...] = jnp.zeros(c_out.shape, F32)
            n_out[...] = jnp.zeros(n_out.shape, F32)
            m_out[...] = jnp.zeros(m_out.shape, F32)
            s_out[...] = jnp.zeros(s_out.shape, F32)

    row_i = lax.broadcasted_iota(jnp.int32, (L, L), 0)
    col_i = lax.broadcasted_iota(jnp.int32, (L, L), 1)
    causal = row_i >= col_i
    tri = causal.astype(F32)
    tri_t = (row_i <= col_i).astype(F32)
    convw = convw_ref[...]
    gn = gn_ref[...]

    def chunk_body(c, carry):
        r0 = pl.multiple_of(c * L, L)
        rows = pl.ds(r0, L)
        si = 0 if carried else c
        tab_rows = rows if carried else pl.ds(0, L)

        u = proj_ref[rows, 0:2 * M_WIDTH]
        if not carried:
            ub_ref[SUBLANES - 3:SUBLANES, :] = conv_in[si]
        ub_ref[SUBLANES:SUBLANES + L, :] = u
        acc = u * convw[CONV_W - 1:CONV_W, :]
        for j in range(CONV_W - 1):
            acc = acc + ub_ref[SUBLANES - 3 + j:SUBLANES - 3 + j + L, :] * convw[j:j + 1, :]
        last_rows = ub_ref[SUBLANES + L - 3:SUBLANES + L, :]
        conv_out[si] = last_rows
        if carried:
            ub_ref[SUBLANES - 3:SUBLANES, :] = last_rows
        qk = acc * _sigmoid(acc)

        gc = gcol_ref[rows, :]
        gr = _dot_nt(wgr_ref[...], xh_ref[rows, :]) + brow_ref[...]
        b_cols = _dot_exact(tri, _log_sigmoid(gc))
        b_rows = _dot_exact(_log_sigmoid(gr), tri_t)

        cos = cos_ref[tab_rows, :]
        sin = sin_ref[tab_rows, :]

        for h in range(M_HEADS):
            lo, hi = h * M_DIM, (h + 1) * M_DIM
            q = qk[:, lo:hi] * (M_DIM ** -0.5)
            k = qk[:, M_WIDTH + lo:M_WIDTH + hi]
            v = proj_ref[rows, 2 * M_WIDTH + lo:2 * M_WIDTH + hi]
            og = proj_ref[rows, 3 * M_WIDTH + lo:3 * M_WIDTH + hi]
            c0 = c_src[si, h]
            n0 = n_src[si, h:h + 1, :]
            m0 = m_src[si, h:h + 1, 0:1]
            b_c = b_cols[:, M_HEADS + h:M_HEADS + h + 1]
            b_r = b_rows[M_HEADS + h:M_HEADS + h + 1, :]
            ig_c = gc[:, h:h + 1]
            ig_r = gr[h:h + 1, :]
            a_c = b_c + m0
            dmat = jnp.where(causal, b_c - b_r + ig_r, -jnp.inf)
            m_c = jnp.maximum(a_c, jnp.max(dmat, axis=-1, keepdims=True))
            w_intra = jnp.exp(dmat - m_c)
            w_inter = jnp.exp(a_c - m_c)
            s = _dot_nt(q, k) * w_intra
            num = _dot(s, v) + w_inter * _dot(q, c0)
            den = jnp.sum(s, axis=-1, keepdims=True) + w_inter * jnp.sum(q * n0, axis=-1, keepdims=True)
            hm = num / jnp.maximum(jnp.abs(den), jnp.exp(-m_c))
            m_last = m_c[L - 1:L, :]
            wk_c = jnp.exp(b_c[L - 1:L, :] - b_c + ig_c - m_last)
            decay = jnp.exp(a_c[L - 1:L, :] - m_last)
            kw = wk_c * k
            c_out[si, h] = decay * c0 + _dot_tn(kw, v)
            n_out[si, h:h + 1, :] = decay * n0 + jnp.sum(kw, axis=0, keepdims=True)
            m_out[si, h:h + 1, :] = jnp.broadcast_to(m_last, (1, LANES))
            hm = _head_norm(hm) * gn[:, lo:hi] * _sigmoid(og)
            hbuf_ref[rows, lo:hi] = hm.astype(hbuf_ref.dtype)

        for h in range(R_HEADS):
            lo, hi = h * R_DIM, (h + 1) * R_DIM
            base = 4 * M_WIDTH
            rq = proj_ref[rows, base + lo:base + hi]
            rk = proj_ref[rows, base + R_WIDTH + lo:base + R_WIDTH + hi]
            v = proj_ref[rows, base + 2 * R_WIDTH + lo:base + 2 * R_WIDTH + hi]
            rg = proj_ref[rows, base + 3 * R_WIDTH + lo:base + 3 * R_WIDTH + hi]
            q = rq * cos + pltpu.roll(rq, R_DIM // 2, 1) * sin
            k = (rk * cos + pltpu.roll(rk, R_DIM // 2, 1) * sin) * (R_DIM ** -0.5)
            s0 = s_src[si, h]
            s = _dot_nt(q, k) * dmask_ref[h]
            hr = _dot(s, v) + _dot(q, s0) * cd_ref[h]
            s_out[si, h] = gl_ref[h] * s0 + _dot_tn(wkr_ref[h] * k, v)
            hr = _head_norm(hr) * gn[:, M_WIDTH + lo:M_WIDTH + hi] * (rg * _sigmoid(rg))
            hbuf_ref[rows, M_WIDTH + lo:M_WIDTH + hi] = hr.astype(hbuf_ref.dtype)
        return carry

    lax.fori_loop(0, n_chunks, chunk_body, 0)

    mix = jnp.dot(hbuf_ref[...], wout_ref[...], preferred_element_type=F32)
    x1_ref[...] = _layer_norm(alpha * xb + mix, lng_ref[...], lnb_ref[...])


def _retention_tables(L):
    log_g = jnp.log(1.0 - 2.0 ** (-5.0 - jnp.arange(R_HEADS, dtype=F32)))
    idx = jnp.arange(L, dtype=F32)
    rel = idx[:, None] - idx[None, :]
    dmask = jnp.where(rel >= 0, jnp.exp(log_g[:, None, None] * jnp.maximum(rel, 0.0)), 0.0)
    cd = jnp.exp(log_g[:, None] * (idx + 1.0))
    wk = jnp.exp(log_g[:, None] * (L - 1.0 - idx))
    gl = jnp.exp(log_g * L)
    bc = lambda t: jnp.broadcast_to(t[:, :, None], (R_HEADS, L, LANES))
    return dmask, bc(cd), bc(wk), jnp.broadcast_to(gl[:, None, None], (R_HEADS, 1, LANES))


def _rotary_tables(pos):
    half = R_DIM // 2
    inv = ROPE_BASE ** (-jnp.arange(half, dtype=F32) / half)
    ang = pos.astype(F32)[:, None] * inv[None, :]
    cos, sin = jnp.cos(ang), jnp.sin(ang)
    return jnp.concatenate([cos, cos], axis=-1), jnp.concatenate([-sin, sin], axis=-1)


def _mixer(x, lw, pos, state, alpha):
    S, T, D = x.shape
    carried = state is None
    if carried:
        block_rows = min(PROMPT_BLOCK_ROWS, T)
        L = min(PROMPT_CHUNK, block_rows)
        seqs = 1
        n_t = T // block_rows
        grid = (S, n_t)
    else:
        seqs = min(SAMPLE_SEQS_PER_BLOCK, S)
        L = T
        block_rows = seqs * T
        n_t = 1
        grid = (S // seqs, 1)
    n_chunks = block_rows // L
    xr = x.reshape(S * T, D)
    cos, sin = _rotary_tables(pos)
    dmask, cd, wkr, gl = _retention_tables(L)

    row_spec = pl.BlockSpec((block_rows, D), lambda b, t: (b * n_t + t, 0))
    tab_spec = pl.BlockSpec((block_rows if carried else T, LANES), lambda b, t: (t, 0))
    in_specs = [
        row_spec,
        _const_spec((D, MAIN_WIDTH)), _const_spec((D, LANES)), _const_spec((N_GATES, D)),
        _const_spec((1, LANES)), _const_spec((N_GATES, 1)), _const_spec((CONV_W, 2 * M_WIDTH)),
        _const_spec((1, D)), _const_spec((D, D)), _const_spec((1, D)), _const_spec((1, D)),
        tab_spec, tab_spec,
        _const_spec((R_HEADS, L, L)), _const_spec((R_HEADS, L, LANES)), _const_spec((R_HEADS, L, LANES)),
        _const_spec((R_HEADS, 1, LANES)),
    ]
    args = [xr, lw['w_main'], lw['w_gc'], lw['w_gr'], lw['b_col'], lw['b_row'], lw['conv_w'],
            lw['gn'], lw['w_out'], lw['ln_g'][0:1], lw['ln_b'][0:1], cos, sin, dmask, cd, wkr, gl]
    state_specs = [
        pl.BlockSpec((seqs, CONV_W - 1, 2 * M_WIDTH), lambda b, t: (b, 0, 0)),
        pl.BlockSpec((seqs, M_HEADS, M_DIM, M_DIM), lambda b, t: (b, 0, 0, 0)),
        pl.BlockSpec((seqs, M_HEADS, M_DIM), lambda b, t: (b, 0, 0)),
        pl.BlockSpec((seqs, M_HEADS, LANES), lambda b, t: (b, 0, 0)),
        pl.BlockSpec((seqs, R_HEADS, R_DIM, R_DIM), lambda b, t: (b, 0, 0, 0)),
    ]
    if not carried:
        in_specs += state_specs
        args += list(state)
    out_shape = [
        jax.ShapeDtypeStruct((S * T, D), F32),
        jax.ShapeDtypeStruct((S, CONV_W - 1, 2 * M_WIDTH), F32),
        jax.ShapeDtypeStruct((S, M_HEADS, M_DIM, M_DIM), F32),
        jax.ShapeDtypeStruct((S, M_HEADS, M_DIM), F32),
        jax.ShapeDtypeStruct((S, M_HEADS, LANES), F32),
        jax.ShapeDtypeStruct((S, R_HEADS, R_DIM, R_DIM), F32),
    ]
    out_specs = [row_spec] + state_specs
    scratch = [
        pltpu.VMEM((block_rows, MAIN_WIDTH), F32),
        pltpu.VMEM((block_rows, LANES), F32),
        pltpu.VMEM((block_rows, D), _MXU_DTYPE),
        pltpu.VMEM((SUBLANES + L, 2 * M_WIDTH), F32),
        pltpu.VMEM((block_rows, D), _MXU_DTYPE),
    ]
    outs = pl.pallas_call(
        functools.partial(_mixer_kernel, chunk=L, n_chunks=n_chunks, carried=carried, alpha=alpha),
        grid=grid, in_specs=in_specs, out_specs=out_specs, out_shape=out_shape,
        scratch_shapes=scratch,
        compiler_params=pltpu.CompilerParams(
            dimension_semantics=("parallel", "arbitrary"), vmem_limit_bytes=VMEM_LIMIT_BYTES),
        name="mixer_prompt" if carried else "mixer_sample",
    )(*args)
    x1, conv_new, c_new, n_new, m_new, s_new = outs
    return x1.reshape(S, T, D), conv_new, c_new, n_new, m_new[:, :, 0], s_new


def _route_rows(logits_t):
    mx = jnp.max(logits_t, axis=0, keepdims=True)
    ex = jnp.exp(logits_t - mx)
    probs = ex / jnp.sum(ex, axis=0, keepdims=True)
    p = [probs[e:e + 1, :] for e in range(N_EXPERTS)]

    def first_max(vals):
        m = vals[0]
        for t in vals[1:]:
            m = jnp.maximum(m, t)
        taken = None
        sel = []
        for t in vals:
            is_max = t == m
            if taken is None:
                pick, taken = is_max, is_max
            else:
                pick = jnp.logical_and(is_max, jnp.logical_not(taken))
                taken = jnp.logical_or(taken, is_max)
            sel.append(pick)
        return m, sel

    scores, picked, denom = [], [], []
    for g in range(N_GROUPS):
        vals = p[g * EXP_PER_GROUP:(g + 1) * EXP_PER_GROUP]
        m1, sel1 = first_max(vals)
        rest = [jnp.where(s1, -1.0, t) for s1, t in zip(sel1, vals)]
        m2, sel2 = first_max(rest)
        scores.append(m1 + m2)
        denom.append(m1 + m2)
        picked.append([jnp.logical_or(a, b) for a, b in zip(sel1, sel2)])
    _, gsel = first_max(scores)
    bucket = jnp.zeros(p[0].shape, jnp.int32)
    ca = jnp.zeros(p[0].shape, F32)
    cb = jnp.zeros(p[0].shape, F32)
    for g in range(N_GROUPS):
        w = [p[g * EXP_PER_GROUP + j] / denom[g] for j in range(EXP_PER_GROUP)]
        for pi, (a, b) in enumerate(PAIRS):
            hit = jnp.logical_and(gsel[g], jnp.logical_and(picked[g][a], picked[g][b]))
            bucket = jnp.where(hit, g * len(PAIRS) + pi, bucket)
            ca = jnp.where(hit, w[a], ca)
            cb = jnp.where(hit, w[b], cb)
    return bucket, ca, cb


def _xattn_kernel(x_ref, k_ref, v_ref, wq_ref, wo_ref, lng_ref, lnb_ref, wr_ref, upper_ref, cnt_in_ref,
                  x2w_ref, rb_ref, cnt_out_ref, q_ref, o_ref, run_ref, *, seqs, seq_rows, alpha):
    @pl.when(jnp.logical_and(pl.program_id(0) == 0, pl.program_id(1) == 0))
    def _():
        run_ref[...] = cnt_in_ref[...]

    xb = x_ref[...]
    q_ref[...] = _dot(xb, wq_ref[...]).astype(q_ref.dtype)
    scale = X_DIM ** -0.5
    for sq in range(seqs):
        r0 = sq * seq_rows
        for h in range(X_HEADS):
            lo, hi = h * X_DIM, (h + 1) * X_DIM
            s = _dot_nt(q_ref[r0:r0 + seq_rows, lo:hi], k_ref[sq, :, lo:hi]) * scale
            e = jnp.exp(s - jnp.max(s, axis=-1, keepdims=True))
            p = e / jnp.sum(e, axis=-1, keepdims=True)
            o_ref[r0:r0 + seq_rows, lo:hi] = _dot(p, v_ref[sq, :, lo:hi]).astype(o_ref.dtype)
    att = jnp.dot(o_ref[...], wo_ref[...], preferred_element_type=F32)
    x2 = _layer_norm(alpha * xb + att, lng_ref[...], lnb_ref[...])
    x2w_ref[:, 0:D_MODEL] = x2
    n = xb.shape[0]

    logits_t = lax.dot_general(wr_ref[...], x2, (((1,), (1,)), ((), ())),
                               preferred_element_type=F32, precision=lax.Precision.HIGHEST)
    bucket, ca, cb = _route_rows(logits_t)

    onehot = (lax.broadcasted_iota(jnp.int32, (BUCKET_ROWS, n), 0) == bucket).astype(F32)
    earlier = jnp.dot(_mx(onehot), upper_ref[...], preferred_element_type=F32)
    run = run_ref[...]
    rank = jnp.sum(onehot * (earlier + run[:, 0:1]), axis=0, keepdims=True)
    run_ref[...] = run + jnp.sum(onehot, axis=1, keepdims=True)
    cnt_out_ref[...] = run_ref[...]
    rb_ref[...] = jnp.concatenate(
        [bucket, rank.astype(jnp.int32), jnp.zeros((SUBLANES - 2, n), jnp.int32)], axis=0)

    cw_t = jnp.concatenate([ca, cb, jnp.zeros((LANES - 2, n), F32)], axis=0)
    x2w_ref[:, D_MODEL:ROW_WIDTH] = cw_t.T


def _xattn(x, mem_k, mem_v, lw, w_router_t, cnt_in, alpha):
    S, T, D = x.shape
    if T >= PROMPT_BLOCK_ROWS:
        seqs, seq_rows = 1, PROMPT_BLOCK_ROWS
    else:
        seqs, seq_rows = min(SAMPLE_SEQS_PER_BLOCK, S), T
    n_t = T // seq_rows
    block_rows = seqs * seq_rows
    grid = (S // seqs, n_t)
    row_map = lambda b, t: (b * n_t + t, 0)
    kv_spec = pl.BlockSpec((seqs, N_MEM, D), lambda b, t: (b, 0, 0))
    idx = jnp.arange(block_rows)
    upper = _mx(idx[:, None] < idx[None, :])
    return pl.pallas_call(
        functools.partial(_xattn_kernel, seqs=seqs, seq_rows=seq_rows, alpha=alpha),
        grid=grid,
        in_specs=[pl.BlockSpec((block_rows, D), row_map), kv_spec, kv_spec,
                  _const_spec((D, D)), _const_spec((D, D)), _const_spec((1, D)), _const_spec((1, D)),
                  _const_spec((N_EXPERTS, D)), _const_spec((block_rows, block_rows)),
                  _const_spec((BUCKET_ROWS, LANES))],
        out_specs=[pl.BlockSpec((block_rows, ROW_WIDTH), row_map),
                   pl.BlockSpec((SUBLANES, block_rows), lambda b, t: (0, b * n_t + t)),
                   _const_spec((BUCKET_ROWS, LANES))],
        out_shape=[jax.ShapeDtypeStruct((S * T, ROW_WIDTH), F32),
                   jax.ShapeDtypeStruct((SUBLANES, S * T), jnp.int32),
                   jax.ShapeDtypeStruct((BUCKET_ROWS, LANES), F32)],
        scratch_shapes=[pltpu.VMEM((block_rows, D), _MXU_DTYPE), pltpu.VMEM((block_rows, D), _MXU_DTYPE),
                        pltpu.VMEM((BUCKET_ROWS, LANES), F32)],
        compiler_params=pltpu.CompilerParams(
            dimension_semantics=("arbitrary", "arbitrary"), vmem_limit_bytes=VMEM_LIMIT_BYTES),
        name="xattn",
    )(x.reshape(S * T, D), mem_k, mem_v, lw['w_xq'], lw['w_xo'], lw['ln_g'][1:2], lw['ln_b'][1:2],
      w_router_t, upper, cnt_in)


def _route_tables(cnt, n_tiles):
    cnt = cnt[:N_BUCKETS, 0].astype(jnp.int32)
    padded = ((cnt + EXPERT_TILE - 1) // EXPERT_TILE) * EXPERT_TILE
    ends = jnp.cumsum(padded)
    base = ends - padded
    tile_start = jnp.arange(n_tiles, dtype=jnp.int32) * EXPERT_TILE
    valid = tile_start < ends[-1]
    last_start = jnp.maximum(ends[-1] - EXPERT_TILE, 0)
    start = jnp.where(valid, tile_start, last_start)
    tb = jnp.sum((ends[None, :] <= start[:, None]).astype(jnp.int32), axis=1)
    tb = jnp.minimum(tb, N_BUCKETS - 1)
    group, pair = tb // len(PAIRS), tb % len(PAIRS)
    pair_a = jnp.asarray([a for a, _ in PAIRS], jnp.int32)
    pair_b = jnp.asarray([b for _, b in PAIRS], jnp.int32)
    ea = group * EXP_PER_GROUP + pair_a[pair]
    eb = group * EXP_PER_GROUP + pair_b[pair]
    used_tiles = (ends[-1:] // EXPERT_TILE).astype(jnp.int32)
    return base, base + cnt, padded - cnt, used_tiles, ea, eb, valid.astype(jnp.int32)


def _dispatch_kernel(pos_ref, pad_start_ref, pad_len_ref, used_tiles_ref, tok_p_ref, tok_s_ref,
                     xs_ref, zero_ref, sem, pad_sem, *, blocks_p):
    i = pl.program_id(0)
    rows = tok_p_ref.shape[0]
    n_tiles = xs_ref.shape[0] // EXPERT_TILE

    def unused_tile(t, wait):
        cp = pltpu.make_async_copy(zero_ref, xs_ref.at[pl.ds(t * EXPERT_TILE, EXPERT_TILE)], pad_sem)
        cp.wait() if wait else cp.start()

    def pad_copies(b, wait):
        off = pad_start_ref[b]
        left = pad_len_ref[b]
        head = jnp.minimum((SUBLANES - off % SUBLANES) % SUBLANES, left)
        for j in range(SUBLANES - 1):
            cp = pltpu.make_async_copy(zero_ref.at[pl.ds(0, 1)], xs_ref.at[pl.ds(off + j, 1)], pad_sem)

            @pl.when(j < head)
            def _():
                cp.wait() if wait else cp.start()
        off = off + head
        left = left - head
        size = ZERO_ROWS
        while size >= SUBLANES:
            dst = xs_ref.at[pl.ds(pl.multiple_of(off, SUBLANES), size)]
            cp = pltpu.make_async_copy(zero_ref.at[pl.ds(0, size)], dst, pad_sem)

            @pl.when(left >= size)
            def _():
                cp.wait() if wait else cp.start()
            take = left >= size
            off = jnp.where(take, off + size, off)
            left = jnp.where(take, left - size, left)
            size //= 2

    @pl.when(i == 0)
    def _():
        zero_ref[...] = jnp.zeros(zero_ref.shape, F32)
        for wait in (False, True):
            lax.fori_loop(used_tiles_ref[0], n_tiles, lambda t, c, w=wait: (unused_tile(t, w), c)[1], 0)
            lax.fori_loop(0, N_BUCKETS, lambda b, c, w=wait: (pad_copies(b, w), c)[1], 0)

    def scatter(tok_ref):
        def issue(k, carry):
            for j in range(DMA_UNROLL):
                r = k * DMA_UNROLL + j
                pltpu.make_async_copy(tok_ref.at[pl.ds(r, 1)], xs_ref.at[pl.ds(pos_ref[i * rows + r], 1)],
                                      sem).start()
            return carry
        lax.fori_loop(0, rows // DMA_UNROLL, issue, 0)
        pltpu.make_async_copy(tok_ref, xs_ref.at[pl.ds(0, rows)], sem).wait()

    @pl.when(i < blocks_p)
    def _():
        scatter(tok_p_ref)

    @pl.when(i >= blocks_p)
    def _():
        scatter(tok_s_ref)


def _dispatch(tok_p, tok_s, pos, pad_start, pad_len, used_tiles, n_rows):
    n_p, n_s = tok_p.shape[0], tok_s.shape[0]
    rows = min(PROMPT_BLOCK_ROWS, n_p, n_s)
    blocks_p, blocks_s = n_p // rows, n_s // rows
    block = (rows, ROW_WIDTH)
    return pl.pallas_call(
        functools.partial(_dispatch_kernel, blocks_p=blocks_p),
        grid_spec=pltpu.PrefetchScalarGridSpec(
            num_scalar_prefetch=4, grid=(blocks_p + blocks_s,),
            in_specs=[pl.BlockSpec(block, lambda i, *_: (jnp.minimum(i, blocks_p - 1), 0)),
                      pl.BlockSpec(block, lambda i, *_: (jnp.maximum(i - blocks_p, 0), 0))],
            out_specs=pl.BlockSpec(memory_space=pl.ANY),
            scratch_shapes=[pltpu.VMEM((EXPERT_TILE, ROW_WIDTH), F32),
                            pltpu.SemaphoreType.DMA, pltpu.SemaphoreType.DMA]),
        out_shape=jax.ShapeDtypeStruct((n_rows, ROW_WIDTH), F32),
        compiler_params=pltpu.CompilerParams(
            dimension_semantics=("arbitrary",), vmem_limit_bytes=VMEM_LIMIT_BYTES),
        name="dispatch",
    )(pos, pad_start, pad_len, used_tiles, tok_p, tok_s)


def _expert_kernel(ea_ref, eb_ref, valid_ref, xs_ref, wga_ref, wua_ref, wda_ref, wgb_ref, wub_ref, wdb_ref,
                   ys_ref, ga_ref, ua_ref, da_ref, gb_ref, ub_ref, db_ref):
    t = pl.program_id(0)

    @pl.when(valid_ref[t] > 0)
    def _():
        prev = jnp.maximum(t - 1, 0)

        @pl.when(jnp.logical_or(t == 0, ea_ref[t] != ea_ref[prev]))
        def _():
            ga_ref[...] = _mx(wga_ref[0])
            ua_ref[...] = _mx(wua_ref[0])
            da_ref[...] = _mx(wda_ref[0])

        @pl.when(jnp.logical_or(t == 0, eb_ref[t] != eb_ref[prev]))
        def _():
            gb_ref[...] = _mx(wgb_ref[0])
            ub_ref[...] = _mx(wub_ref[0])
            db_ref[...] = _mx(wdb_ref[0])

        x = _mx(xs_ref[:, 0:D_MODEL])
        cw = xs_ref[:, D_MODEL:ROW_WIDTH]

        def ffn(g_ref, u_ref, d_ref):
            gate = jnp.dot(x, g_ref[...], preferred_element_type=F32)
            up = jnp.dot(x, u_ref[...], preferred_element_type=F32)
            return _dot(gate * _sigmoid(gate) * up, d_ref[...])
        y = cw[:, 0:1] * ffn(ga_ref, ua_ref, da_ref) + cw[:, 1:2] * ffn(gb_ref, ub_ref, db_ref)
        for s in range(FEATURE_ROWS):
            ys_ref[pl.ds(s, EXPERT_TILE, stride=FEATURE_ROWS), :] = y[:, s * LANES:(s + 1) * LANES]

    @pl.when(valid_ref[t] == 0)
    def _():
        ys_ref[...] = jnp.zeros(ys_ref.shape, F32)


def _experts(xs, ea, eb, valid, w_e_gate, w_e_up, w_e_down):
    n_tiles = ea.shape[0]
    D = D_MODEL
    wa = lambda t, ea, eb, v: (ea[t], 0, 0)
    wb = lambda t, ea, eb, v: (eb[t], 0, 0)
    gu = (1, D, D_EXPERT)
    dn = (1, D_EXPERT, D)
    return pl.pallas_call(
        _expert_kernel,
        grid_spec=pltpu.PrefetchScalarGridSpec(
            num_scalar_prefetch=3, grid=(n_tiles,),
            in_specs=[pl.BlockSpec((EXPERT_TILE, ROW_WIDTH), lambda t, ea, eb, v: (t * v[t], 0)),
                      pl.BlockSpec(gu, wa), pl.BlockSpec(gu, wa), pl.BlockSpec(dn, wa),
                      pl.BlockSpec(gu, wb), pl.BlockSpec(gu, wb), pl.BlockSpec(dn, wb)],
            out_specs=pl.BlockSpec((EXPERT_TILE * FEATURE_ROWS, LANES), lambda t, ea, eb, v: (t, 0)),
            scratch_shapes=[pltpu.VMEM((D, D_EXPERT), _MXU_DTYPE), pltpu.VMEM((D, D_EXPERT), _MXU_DTYPE),
                            pltpu.VMEM((D_EXPERT, D), _MXU_DTYPE), pltpu.VMEM((D, D_EXPERT), _MXU_DTYPE),
                            pltpu.VMEM((D, D_EXPERT), _MXU_DTYPE), pltpu.VMEM((D_EXPERT, D), _MXU_DTYPE)]),
        out_shape=jax.ShapeDtypeStruct((n_tiles * EXPERT_TILE * FEATURE_ROWS, LANES), F32),
        compiler_params=pltpu.CompilerParams(
            dimension_semantics=("arbitrary",), vmem_limit_bytes=VMEM_LIMIT_BYTES),
        name="experts",
    )(ea, eb, valid, xs, w_e_gate, w_e_up, w_e_down, w_e_gate, w_e_up, w_e_down)


def _combine_kernel(pos_ref, x_ref, ys_ref, lng_ref, lnb_ref, out_ref, buf0_ref, buf1_ref, sem, *, alpha):
    i = pl.program_id(0)
    n_steps = pl.num_programs(0)
    half = x_ref.shape[0] // 2
    bufs = (buf0_ref, buf1_ref)

    def fetch(block, slot):
        def issue(k, carry):
            for j in range(DMA_UNROLL):
                r = k * DMA_UNROLL + j
                src = pl.multiple_of(pos_ref[block * half + r] * FEATURE_ROWS, FEATURE_ROWS)
                pltpu.make_async_copy(ys_ref.at[pl.ds(src, FEATURE_ROWS)],
                                      bufs[slot].at[pl.ds(r * FEATURE_ROWS, FEATURE_ROWS)], sem.at[slot]).start()
            return carry
        lax.fori_loop(0, half // DMA_UNROLL, issue, 0)

    def finish(slot):
        buf = bufs[slot]
        pltpu.make_async_copy(ys_ref.at[pl.ds(0, half * FEATURE_ROWS)], buf, sem.at[slot]).wait()
        y = jnp.concatenate([buf[pl.ds(s, half, stride=FEATURE_ROWS), :] for s in range(FEATURE_ROWS)], axis=-1)
        rows = pl.ds(slot * half, half)
        out_ref[rows, :] = _layer_norm(alpha * x_ref[rows, :] + y, lng_ref[...], lnb_ref[...])

    @pl.when(i == 0)
    def _():
        fetch(0, 0)

    fetch(2 * i + 1, 1)
    finish(0)

    @pl.when(i + 1 < n_steps)
    def _():
        fetch(2 * i + 2, 0)

    finish(1)


def _combine(x2w, ys, pos, lw, alpha):
    n, D = x2w.shape[0], D_MODEL
    rows = min(2 * PROMPT_BLOCK_ROWS, n)
    row_map = lambda i, pos: (i, 0)
    half_buf = pltpu.VMEM((rows // 2 * FEATURE_ROWS, LANES), F32)
    return pl.pallas_call(
        functools.partial(_combine_kernel, alpha=alpha),
        grid_spec=pltpu.PrefetchScalarGridSpec(
            num_scalar_prefetch=1, grid=(n // rows,),
            in_specs=[pl.BlockSpec((rows, D), row_map), pl.BlockSpec(memory_space=pl.ANY),
                      pl.BlockSpec((1, D), lambda i, pos: (0, 0)), pl.BlockSpec((1, D), lambda i, pos: (0, 0))],
            out_specs=pl.BlockSpec((rows, D), row_map),
            scratch_shapes=[half_buf, half_buf, pltpu.SemaphoreType.DMA((2,))]),
        out_shape=jax.ShapeDtypeStruct((n, D), F32),
        compiler_params=pltpu.CompilerParams(
            dimension_semantics=("arbitrary",), vmem_limit_bytes=VMEM_LIMIT_BYTES),
        name="combine",
    )(pos, x2w, ys, lw['ln_g'][2:3], lw['ln_b'][2:3])


def _moe(x2_p, rb_p, x2_s, rb_s, cnt, experts, layer, lw, alpha):
    n_p, n_s = x2_p.shape[0], x2_s.shape[0]
    n_tiles = -(-(n_p + n_s + N_BUCKETS * (EXPERT_TILE - 1)) // EXPERT_TILE)
    base, pad_start, pad_len, used_tiles, ea, eb, valid = _route_tables(cnt, n_tiles)
    pos_p = base[rb_p[0]] + rb_p[1]
    pos_s = base[rb_s[0]] + rb_s[1]
    xs = _dispatch(x2_p, x2_s, jnp.concatenate([pos_p, pos_s]), pad_start, pad_len, used_tiles,
                   n_tiles * EXPERT_TILE)
    ys = _experts(xs, ea + layer * N_EXPERTS, eb + layer * N_EXPERTS, valid, *experts)
    return _combine(x2_p, ys, pos_p, lw, alpha), _combine(x2_s, ys, pos_s, lw, alpha)


def _memkv_kernel(mem_ref, wk_ref, wv_ref, k_ref, v_ref, kh_ref, vh_ref):
    mh = _mx(mem_ref[...])
    k = jnp.dot(mh, wk_ref[...], preferred_element_type=F32)
    v = jnp.dot(mh, wv_ref[...], preferred_element_type=F32)
    k_ref[...] = k
    v_ref[...] = v
    kh_ref[...] = k.astype(kh_ref.dtype)
    vh_ref[...] = v.astype(vh_ref.dtype)


def _mem_kv(mem, lw):
    n, D = mem.shape
    rows = min(PROMPT_BLOCK_ROWS, n)
    spec = pl.BlockSpec((rows, D), lambda i: (i, 0))
    return pl.pallas_call(
        _memkv_kernel,
        grid=(n // rows,),
        in_specs=[spec, _const_spec((D, D)), _const_spec((D, D))],
        out_specs=[spec, spec, spec, spec],
        out_shape=[jax.ShapeDtypeStruct((n, D), F32), jax.ShapeDtypeStruct((n, D), F32),
                   jax.ShapeDtypeStruct((n, D), _MXU_DTYPE), jax.ShapeDtypeStruct((n, D), _MXU_DTYPE)],
        compiler_params=pltpu.CompilerParams(
            dimension_semantics=("parallel",), vmem_limit_bytes=VMEM_LIMIT_BYTES),
        name="mem_kv",
    )(mem, lw['w_xk'], lw['w_xv'])


def _layer_weights(l, w_in, b_if, conv_w, gn_m, gn_r, w_out, w_xq, w_xk, w_xv, w_xo, ln_g, ln_b):
    wi = w_in[l]
    gates = wi[:, GATE_COL0:GATE_COL0 + N_GATES]
    b = b_if[l].astype(F32)
    return {
        'w_main': _mx(jnp.concatenate([wi[:, :GATE_COL0], wi[:, GATE_COL0 + N_GATES:]], axis=1)),
        'w_gc': _mx(jnp.pad(gates, ((0, 0), (0, LANES - N_GATES)))),
        'w_gr': _mx(gates.T),
        'b_col': jnp.pad(b, (0, LANES - N_GATES))[None, :],
        'b_row': b[:, None],
        'conv_w': conv_w[l].astype(F32),
        'gn': jnp.concatenate([gn_m[l], gn_r[l]]).astype(F32)[None, :],
        'w_out': _mx(w_out[l]), 'w_xq': _mx(w_xq[l]), 'w_xk': _mx(w_xk[l]), 'w_xv': _mx(w_xv[l]),
        'w_xo': _mx(w_xo[l]),
        'ln_g': ln_g[l].astype(F32), 'ln_b': ln_b[l].astype(F32),
    }


def kernel(x_prompt, x_sample, mem_prompt, cache_mem_k, cache_mem_v, state_conv, state_mlstm_C,
           state_mlstm_n, state_mlstm_m, state_ret_S, w_in, b_if, conv_w, gn_m, gn_r, w_out, w_xq,
           w_xk, w_xv, w_xo, w_router, w_e_gate, w_e_up, w_e_down, ln_g, ln_b):
    depth = w_in.shape[0]
    alpha = (2 * depth) ** 0.25
    Bp, Tp, D = x_prompt.shape
    Bs, Ts, _ = x_sample.shape
    pos_p = jnp.arange(Tp)
    pos_s = PAST_LEN + jnp.arange(Ts)
    w_router_t = w_router.astype(F32).T
    mem2d = mem_prompt.reshape(Bp * N_MEM, D)
    xp, xs = x_prompt, x_sample
    experts = tuple(w.reshape(depth * N_EXPERTS, *w.shape[2:]) for w in (w_e_gate, w_e_up, w_e_down))
    cnt0 = jnp.zeros((BUCKET_ROWS, LANES), F32)
    outs = [[] for _ in range(12)]
    for l in range(depth):
        lw = _layer_weights(l, w_in, b_if, conv_w, gn_m, gn_r, w_out, w_xq, w_xk, w_xv, w_xo, ln_g, ln_b)
        kp, vp, kph, vph = _mem_kv(mem2d, lw)
        x1, cb, c_n, n_n, m_n, s_n = _mixer(xp, lw, pos_p, None, alpha)
        x2_p, rb_p, cnt = _xattn(x1, kph.reshape(Bp, N_MEM, D), vph.reshape(Bp, N_MEM, D), lw,
                                        w_router_t, cnt0, alpha)
        for lst, val in zip(outs[:7], (kp.reshape(Bp, N_MEM, X_HEADS, X_DIM),
                                       vp.reshape(Bp, N_MEM, X_HEADS, X_DIM), cb, c_n, n_n, m_n, s_n)):
            lst.append(val)
        state = (state_conv[l], state_mlstm_C[l], state_mlstm_n[l],
                 jnp.broadcast_to(state_mlstm_m[l][:, :, None], (Bs, M_HEADS, LANES)), state_ret_S[l])
        x1, cb, c_n, n_n, m_n, s_n = _mixer(xs, lw, pos_s, state, alpha)
        x2_s, rb_s, cnt = _xattn(x1, _mx(cache_mem_k[l].reshape(Bs, N_MEM, D)),
                                        _mx(cache_mem_v[l].reshape(Bs, N_MEM, D)), lw, w_router_t, cnt, alpha)
        for lst, val in zip(outs[7:], (cb, c_n, n_n, m_n, s_n)):
            lst.append(val)
        xp2, xs2 = _moe(x2_p, rb_p, x2_s, rb_s, cnt, experts, l, lw, alpha)
        xp, xs = xp2.reshape(Bp, Tp, D), xs2.reshape(Bs, Ts, D)
    return (xp, xs) + tuple(jnp.stack(o) for o in outs)
```

```python
import functools

import jax
import jax.numpy as jnp
from jax import lax
from jax.experimental import pallas as pl
from jax.experimental.pallas import tpu as pltpu

F32 = jnp.float32
_MXU_DTYPE = jnp.bfloat16

D_MODEL = 1024
PAST_LEN = 4096
N_MEM = 256
M_WIDTH = D_MODEL // 2
M_HEADS = 4
M_DIM = M_WIDTH // M_HEADS
R_WIDTH = D_MODEL - M_WIDTH
R_HEADS = 4
R_DIM = R_WIDTH // R_HEADS
CONV_W = 4
ROPE_BASE = 10000.0
X_HEADS = 4
X_DIM = D_MODEL // X_HEADS
N_EXPERTS = 16
N_GROUPS = 4
EXP_PER_GROUP = N_EXPERTS // N_GROUPS
D_EXPERT = D_MODEL // 2
LN_EPS = 1e-5
GATE_COL0 = 2 * M_WIDTH + 2 * M_WIDTH
N_GATES = 2 * M_HEADS
MAIN_WIDTH = 4 * D_MODEL

LANES = 128
SUBLANES = 8
VMEM_LIMIT_BYTES = 56 * 1024 * 1024

PROMPT_BLOCK_ROWS = 512
PROMPT_CHUNK = 256
SAMPLE_SEQS_PER_BLOCK = 8

PAIRS = ((0, 1), (0, 2), (0, 3), (1, 2), (1, 3), (2, 3))
N_BUCKETS = N_GROUPS * len(PAIRS)
BUCKET_ROWS = 32
EXPERT_TILE = 256
ROW_WIDTH = D_MODEL + LANES
FEATURE_ROWS = D_MODEL // LANES
DMA_UNROLL = 8
HEAD_GROUPS = ((('m', 0), ('m', 1)), (('r', 0), ('r', 1)), (('m', 2), ('m', 3)), (('r', 2), ('r', 3)))
SAMPLE_SEQS_INTERLEAVED = 2
XATTN_SEQ_GROUP = 4
ZERO_ROWS = EXPERT_TILE // 2


def _mx(a):
    return a.astype(_MXU_DTYPE)


def _dot(a, b):
    return jnp.dot(_mx(a), _mx(b), preferred_element_type=F32)


def _dot_nt(a, b):
    return lax.dot_general(_mx(a), _mx(b), (((1,), (1,)), ((), ())), preferred_element_type=F32)


def _dot_tn(a, b):
    return lax.dot_general(_mx(a), _mx(b), (((0,), (0,)), ((), ())), preferred_element_type=F32)


def _dot_exact(a, b):
    return jnp.dot(a, b, preferred_element_type=F32, precision=lax.Precision.HIGHEST)


def _layer_norm(y, g, b):
    mu = jnp.mean(y, axis=-1, keepdims=True)
    yc = y - mu
    var = jnp.mean(yc * yc, axis=-1, keepdims=True)
    return yc * lax.rsqrt(var + LN_EPS) * g + b


def _head_norm(h):
    mu = jnp.mean(h, axis=-1, keepdims=True)
    hc = h - mu
    var = jnp.mean(hc * hc, axis=-1, keepdims=True)
    return hc * lax.rsqrt(var + LN_EPS)


def _sigmoid(x):
    return 1.0 / (1.0 + jnp.exp(-x))


def _log_sigmoid(x):
    return jnp.minimum(x, 0.0) - jnp.log1p(jnp.exp(-jnp.abs(x)))


def _run_interleaved(gens):
    live = list(gens)
    while live:
        for g in list(live):
            try:
                next(g)
            except StopIteration:
                live.remove(g)


def _const_spec(shape):
    n = len(shape)
    return pl.BlockSpec(shape, lambda *_: (0,) * n)


def _mixer_kernel(*refs, chunk, n_chunks, carried, alpha):
    (x_ref, wmain_ref, wgc_ref, wgr_ref, bcol_ref, brow_ref, convw_ref, gn_ref, wout_ref,
     lng_ref, lnb_ref, cos_ref, sin_ref, dmask_ref, cd_ref, wkr_ref, gl_ref) = refs[:17]
    if carried:
        state_in = None
        rest = refs[17:]
    else:
        state_in = refs[17:22]
        rest = refs[22:]
    x1_ref, conv_out, c_out, n_out, m_out, s_out = rest[:6]
    proj_ref, gcol_ref, hbuf_ref, ub_ref, xh_ref = rest[6:]
    if carried:
        conv_in = None
        c_src, n_src, m_src, s_src = c_out, n_out, m_out, s_out
    else:
        conv_in, c_src, n_src, m_src, s_src = state_in
    L = chunk

    xb = x_ref[...]
    xh = _mx(xb)
    xh_ref[...] = xh
    proj_ref[...] = jnp.dot(xh, wmain_ref[...], preferred_element_type=F32)
    gcol_ref[...] = jnp.dot(xh, wgc_ref[...], preferred_element_type=F32) + bcol_ref[...]

    if carried:
        @pl.when(pl.program_id(1) == 0)
        def _():
            ub_ref[0:SUBLANES, :] = jnp.zeros((SUBLANES, D_MODEL), F32)
            c_out[...] = jnp.zeros(c_out.shape, F32)
            n_out[...] = jnp.zeros(n_out.shape, F32)
            m_out[...] = jnp.zeros(m_out.shape, F32)
            s_out[...] = jnp.zeros(s_out.shape, F32)

    row_i = lax.broadcasted_iota(jnp.int32, (L, L), 0)
    col_i = lax.broadcasted_iota(jnp.int32, (L, L), 1)
    causal = row_i >= col_i
    tri = causal.astype(F32)
    tri_t = (row_i <= col_i).astype(F32)
    convw = convw_ref[...]
    gn = gn_ref[...]

    def chunk_heads(c):
        r0 = pl.multiple_of(c * L, L)
        rows = pl.ds(r0, L)
        si = 0 if carried else c
        tab_rows = rows if carried else pl.ds(0, L)

        u = proj_ref[rows, 0:2 * M_WIDTH]
        if not carried:
            ub_ref[SUBLANES - 3:SUBLANES, :] = conv_in[si]
        ub_ref[SUBLANES:SUBLANES + L, :] = u
        acc = u * convw[CONV_W - 1:CONV_W, :]
        for j in range(CONV_W - 1):
            acc = acc + ub_ref[SUBLANES - 3 + j:SUBLANES - 3 + j + L, :] * convw[j:j + 1, :]
        last_rows = ub_ref[SUBLANES + L - 3:SUBLANES + L, :]
        conv_out[si] = last_rows
        if carried:
            ub_ref[SUBLANES - 3:SUBLANES, :] = last_rows
        qk = acc * _sigmoid(acc)

        gc = gcol_ref[rows, :]
        gr = _dot_nt(wgr_ref[...], xh_ref[rows, :]) + brow_ref[...]
        b_cols = _dot_exact(tri, _log_sigmoid(gc))
        b_rows = _dot_exact(_log_sigmoid(gr), tri_t)

        cos = cos_ref[tab_rows, :]
        sin = sin_ref[tab_rows, :]

        def mlstm_head(h):
            lo, hi = h * M_DIM, (h + 1) * M_DIM
            q = qk[:, lo:hi] * (M_DIM ** -0.5)
            k = qk[:, M_WIDTH + lo:M_WIDTH + hi]
            v = proj_ref[rows, 2 * M_WIDTH + lo:2 * M_WIDTH + hi]
            c0 = c_src[si, h]
            n0 = n_src[si, h:h + 1, :]
            m0 = m_src[si, h:h + 1, 0:1]
            b_c = b_cols[:, M_HEADS + h:M_HEADS + h + 1]
            b_r = b_rows[M_HEADS + h:M_HEADS + h + 1, :]
            ig_c = gc[:, h:h + 1]
            ig_r = gr[h:h + 1, :]
            a_c = b_c + m0
            s_raw = _dot_nt(q, k)
            yield
            qc = _dot(q, c0)
            yield
            dmat = jnp.where(causal, b_c - b_r + ig_r, -jnp.inf)
            yield
            m_c = jnp.maximum(a_c, jnp.max(dmat, axis=-1, keepdims=True))
            yield
            w_intra = jnp.exp(dmat - m_c)
            yield
            w_inter = jnp.exp(a_c - m_c)
            s = s_raw * w_intra
            yield
            sv = _dot(s, v)
            yield
            m_last = m_c[L - 1:L, :]
            wk_c = jnp.exp(b_c[L - 1:L, :] - b_c + ig_c - m_last)
            decay = jnp.exp(a_c[L - 1:L, :] - m_last)
            kw = wk_c * k
            upd = _dot_tn(kw, v)
            yield
            num = sv + w_inter * qc
            den = jnp.sum(s, axis=-1, keepdims=True) + w_inter * jnp.sum(q * n0, axis=-1, keepdims=True)
            yield
            hm = num / jnp.maximum(jnp.abs(den), jnp.exp(-m_c))
            yield
            c_out[si, h] = decay * c0 + upd
            n_out[si, h:h + 1, :] = decay * n0 + jnp.sum(kw, axis=0, keepdims=True)
            m_out[si, h:h + 1, :] = jnp.broadcast_to(m_last, (1, LANES))
            yield
            og = proj_ref[rows, 3 * M_WIDTH + lo:3 * M_WIDTH + hi]
            hm = _head_norm(hm) * gn[:, lo:hi] * _sigmoid(og)
            hbuf_ref[rows, lo:hi] = hm.astype(hbuf_ref.dtype)
            yield

        def retention_head(h):
            lo, hi = h * R_DIM, (h + 1) * R_DIM
            base = 4 * M_WIDTH
            rq = proj_ref[rows, base + lo:base + hi]
            rk = proj_ref[rows, base + R_WIDTH + lo:base + R_WIDTH + hi]
            v = proj_ref[rows, base + 2 * R_WIDTH + lo:base + 2 * R_WIDTH + hi]
            q = rq * cos + pltpu.roll(rq, R_DIM // 2, 1) * sin
            yield
            k = (rk * cos + pltpu.roll(rk, R_DIM // 2, 1) * sin) * (R_DIM ** -0.5)
            s0 = s_src[si, h]
            yield
            s_raw = _dot_nt(q, k)
            yield
            qs = _dot(q, s0)
            yield
            upd = _dot_tn(wkr_ref[h] * k, v)
            yield
            s = s_raw * dmask_ref[h]
            yield
            sv = _dot(s, v)
            yield
            hr = sv + qs * cd_ref[h]
            s_out[si, h] = gl_ref[h] * s0 + upd
            yield
            rg = proj_ref[rows, base + 3 * R_WIDTH + lo:base + 3 * R_WIDTH + hi]
            hr = _head_norm(hr) * gn[:, M_WIDTH + lo:M_WIDTH + hi] * (rg * _sigmoid(rg))
            hbuf_ref[rows, M_WIDTH + lo:M_WIDTH + hi] = hr.astype(hbuf_ref.dtype)
            yield

        return [[(mlstm_head if kind == 'm' else retention_head)(h) for kind, h in group]
                for group in HEAD_GROUPS]

    seq_par = 1 if carried else SAMPLE_SEQS_INTERLEAVED

    def chunk_body(i, carry):
        per_chunk = [chunk_heads(i * seq_par + j) for j in range(seq_par)]
        for groups in zip(*per_chunk):
            _run_interleaved([g for group in groups for g in group])
        return carry

    lax.fori_loop(0, n_chunks // seq_par, chunk_body, 0)

    mix = jnp.dot(hbuf_ref[...], wout_ref[...], preferred_element_type=F32)
    x1_ref[...] = _layer_norm(alpha * xb + mix, lng_ref[...], lnb_ref[...])


def _retention_tables(L):
    log_g = jnp.log(1.0 - 2.0 ** (-5.0 - jnp.arange(R_HEADS, dtype=F32)))
    idx = jnp.arange(L, dtype=F32)
    rel = idx[:, None] - idx[None, :]
    dmask = jnp.where(rel >= 0, jnp.exp(log_g[:, None, None] * jnp.maximum(rel, 0.0)), 0.0)
    cd = jnp.exp(log_g[:, None] * (idx + 1.0))
    wk = jnp.exp(log_g[:, None] * (L - 1.0 - idx))
    gl = jnp.exp(log_g * L)
    bc = lambda t: jnp.broadcast_to(t[:, :, None], (R_HEADS, L, LANES))
    return dmask, bc(cd), bc(wk), jnp.broadcast_to(gl[:, None, None], (R_HEADS, 1, LANES))


def _rotary_tables(pos):
    half = R_DIM // 2
    inv = ROPE_BASE ** (-jnp.arange(half, dtype=F32) / half)
    ang = pos.astype(F32)[:, None] * inv[None, :]
    cos, sin = jnp.cos(ang), jnp.sin(ang)
    return jnp.concatenate([cos, cos], axis=-1), jnp.concatenate([-sin, sin], axis=-1)


def _mixer(x, lw, pos, state, alpha):
    S, T, D = x.shape
    carried = state is None
    if carried:
        block_rows = min(PROMPT_BLOCK_ROWS, T)
        L = min(PROMPT_CHUNK, block_rows)
        seqs = 1
        n_t = T // block_rows
        grid = (S, n_t)
    else:
        seqs = min(SAMPLE_SEQS_PER_BLOCK, S)
        L = T
        block_rows = seqs * T
        n_t = 1
        grid = (S // seqs, 1)
    n_chunks = block_rows // L
    xr = x.reshape(S * T, D)
    cos, sin = _rotary_tables(pos)
    dmask, cd, wkr, gl = _retention_tables(L)

    row_spec = pl.BlockSpec((block_rows, D), lambda b, t: (b * n_t + t, 0))
    tab_spec = pl.BlockSpec((block_rows if carried else T, LANES), lambda b, t: (t, 0))
    in_specs = [
        row_spec,
        _const_spec((D, MAIN_WIDTH)), _const_spec((D, LANES)), _const_spec((N_GATES, D)),
        _const_spec((1, LANES)), _const_spec((N_GATES, 1)), _const_spec((CONV_W, 2 * M_WIDTH)),
        _const_spec((1, D)), _const_spec((D, D)), _const_spec((1, D)), _const_spec((1, D)),
        tab_spec, tab_spec,
        _const_spec((R_HEADS, L, L)), _const_spec((R_HEADS, L, LANES)), _const_spec((R_HEADS, L, LANES)),
        _const_spec((R_HEADS, 1, LANES)),
    ]
    args = [xr, lw['w_main'], lw['w_gc'], lw['w_gr'], lw['b_col'], lw['b_row'], lw['conv_w'],
            lw['gn'], lw['w_out'], lw['ln_g'][0:1], lw['ln_b'][0:1], cos, sin, dmask, cd, wkr, gl]
    state_specs = [
        pl.BlockSpec((seqs, CONV_W - 1, 2 * M_WIDTH), lambda b, t: (b, 0, 0)),
        pl.BlockSpec((seqs, M_HEADS, M_DIM, M_DIM), lambda b, t: (b, 0, 0, 0)),
        pl.BlockSpec((seqs, M_HEADS, M_DIM), lambda b, t: (b, 0, 0)),
        pl.BlockSpec((seqs, M_HEADS, LANES), lambda b, t: (b, 0, 0)),
        pl.BlockSpec((seqs, R_HEADS, R_DIM, R_DIM), lambda b, t: (b, 0, 0, 0)),
    ]
    if not carried:
        in_specs += state_specs
        args += list(state)
    out_shape = [
        jax.ShapeDtypeStruct((S * T, D), F32),
        jax.ShapeDtypeStruct((S, CONV_W - 1, 2 * M_WIDTH), F32),
        jax.ShapeDtypeStruct((S, M_HEADS, M_DIM, M_DIM), F32),
        jax.ShapeDtypeStruct((S, M_HEADS, M_DIM), F32),
        jax.ShapeDtypeStruct((S, M_HEADS, LANES), F32),
        jax.ShapeDtypeStruct((S, R_HEADS, R_DIM, R_DIM), F32),
    ]
    out_specs = [row_spec] + state_specs
    scratch = [
        pltpu.VMEM((block_rows, MAIN_WIDTH), F32),
        pltpu.VMEM((block_rows, LANES), F32),
        pltpu.VMEM((block_rows, D), _MXU_DTYPE),
        pltpu.VMEM((SUBLANES + L, 2 * M_WIDTH), F32),
        pltpu.VMEM((block_rows, D), _MXU_DTYPE),
    ]
    outs = pl.pallas_call(
        functools.partial(_mixer_kernel, chunk=L, n_chunks=n_chunks, carried=carried, alpha=alpha),
        grid=grid, in_specs=in_specs, out_specs=out_specs, out_shape=out_shape,
        scratch_shapes=scratch,
        compiler_params=pltpu.CompilerParams(
            dimension_semantics=("parallel", "arbitrary"), vmem_limit_bytes=VMEM_LIMIT_BYTES),
        name="mixer_prompt" if carried else "mixer_sample",
    )(*args)
    x1, conv_new, c_new, n_new, m_new, s_new = outs
    return x1.reshape(S, T, D), conv_new, c_new, n_new, m_new[:, :, 0], s_new


def _route_rows(logits_t):
    mx = jnp.max(logits_t, axis=0, keepdims=True)
    ex = jnp.exp(logits_t - mx)
    probs = ex / jnp.sum(ex, axis=0, keepdims=True)
    p = [probs[e:e + 1, :] for e in range(N_EXPERTS)]

    def first_max(vals):
        m = vals[0]
        for t in vals[1:]:
            m = jnp.maximum(m, t)
        taken = None
        sel = []
        for t in vals:
            is_max = t == m
            if taken is None:
                pick, taken = is_max, is_max
            else:
                pick = jnp.logical_and(is_max, jnp.logical_not(taken))
                taken = jnp.logical_or(taken, is_max)
            sel.append(pick)
        return m, sel

    scores, picked, denom = [], [], []
    for g in range(N_GROUPS):
        vals = p[g * EXP_PER_GROUP:(g + 1) * EXP_PER_GROUP]
        m1, sel1 = first_max(vals)
        rest = [jnp.where(s1, -1.0, t) for s1, t in zip(sel1, vals)]
        m2, sel2 = first_max(rest)
        scores.append(m1 + m2)
        denom.append(m1 + m2)
        picked.append([jnp.logical_or(a, b) for a, b in zip(sel1, sel2)])
    _, gsel = first_max(scores)
    bucket = jnp.zeros(p[0].shape, jnp.int32)
    ca = jnp.zeros(p[0].shape, F32)
    cb = jnp.zeros(p[0].shape, F32)
    for g in range(N_GROUPS):
        w = [p[g * EXP_PER_GROUP + j] / denom[g] for j in range(EXP_PER_GROUP)]
        for pi, (a, b) in enumerate(PAIRS):
            hit = jnp.logical_and(gsel[g], jnp.logical_and(picked[g][a], picked[g][b]))
            bucket = jnp.where(hit, g * len(PAIRS) + pi, bucket)
            ca = jnp.where(hit, w[a], ca)
            cb = jnp.where(hit, w[b], cb)
    return bucket, ca, cb


def _xattn_kernel(x_ref, k_ref, v_ref, wq_ref, wo_ref, lng_ref, lnb_ref, wr_ref, upper_ref, cnt_in_ref,
                  x2w_ref, rb_ref, cnt_out_ref, q_ref, o_ref, run_ref, *, seqs, seq_rows, alpha):
    @pl.when(jnp.logical_and(pl.program_id(0) == 0, pl.program_id(1) == 0))
    def _():
        run_ref[...] = cnt_in_ref[...]

    xb = x_ref[...]
    q_ref[...] = _dot(xb, wq_ref[...]).astype(q_ref.dtype)
    scale = X_DIM ** -0.5
    def attend(sq, h):
        r0 = sq * seq_rows
        lo, hi = h * X_DIM, (h + 1) * X_DIM
        s = _dot_nt(q_ref[r0:r0 + seq_rows, lo:hi], k_ref[sq, :, lo:hi]) * scale
        yield
        e = jnp.exp(s - jnp.max(s, axis=-1, keepdims=True))
        yield
        p = e / jnp.sum(e, axis=-1, keepdims=True)
        yield
        o_ref[r0:r0 + seq_rows, lo:hi] = _dot(p, v_ref[sq, :, lo:hi]).astype(o_ref.dtype)

    for sq0 in range(0, seqs, XATTN_SEQ_GROUP):
        _run_interleaved([attend(sq, h) for sq in range(sq0, min(sq0 + XATTN_SEQ_GROUP, seqs))
                          for h in range(X_HEADS)])
    att = jnp.dot(o_ref[...], wo_ref[...], preferred_element_type=F32)
    x2 = _layer_norm(alpha * xb + att, lng_ref[...], lnb_ref[...])
    x2w_ref[:, 0:D_MODEL] = x2
    n = xb.shape[0]

    logits_t = lax.dot_general(wr_ref[...], x2, (((1,), (1,)), ((), ())),
                               preferred_element_type=F32, precision=lax.Precision.HIGHEST)
    bucket, ca, cb = _route_rows(logits_t)

    onehot = (lax.broadcasted_iota(jnp.int32, (BUCKET_ROWS, n), 0) == bucket).astype(F32)
    earlier = jnp.dot(_mx(onehot), upper_ref[...], preferred_element_type=F32)
    run = run_ref[...]
    rank = jnp.sum(onehot * (earlier + run[:, 0:1]), axis=0, keepdims=True)
    run_ref[...] = run + jnp.sum(onehot, axis=1, keepdims=True)
    cnt_out_ref[...] = run_ref[...]
    rb_ref[...] = jnp.concatenate(
        [bucket, rank.astype(jnp.int32), jnp.zeros((SUBLANES - 2, n), jnp.int32)], axis=0)

    cw_t = jnp.concatenate([ca, cb, jnp.zeros((LANES - 2, n), F32)], axis=0)
    x2w_ref[:, D_MODEL:ROW_WIDTH] = cw_t.T


def _xattn(x, mem_k, mem_v, lw, w_router_t, cnt_in, alpha):
    S, T, D = x.shape
    if T >= PROMPT_BLOCK_ROWS:
        seqs, seq_rows = 1, PROMPT_BLOCK_ROWS
    else:
        seqs, seq_rows = min(SAMPLE_SEQS_PER_BLOCK, S), T
    n_t = T // seq_rows
    block_rows = seqs * seq_rows
    grid = (S // seqs, n_t)
    row_map = lambda b, t: (b * n_t + t, 0)
    kv_spec = pl.BlockSpec((seqs, N_MEM, D), lambda b, t: (b, 0, 0))
    idx = jnp.arange(block_rows)
    upper = _mx(idx[:, None] < idx[None, :])
    return pl.pallas_call(
        functools.partial(_xattn_kernel, seqs=seqs, seq_rows=seq_rows, alpha=alpha),
        grid=grid,
        in_specs=[pl.BlockSpec((block_rows, D), row_map), kv_spec, kv_spec,
                  _const_spec((D, D)), _const_spec((D, D)), _const_spec((1, D)), _const_spec((1, D)),
                  _const_spec((N_EXPERTS, D)), _const_spec((block_rows, block_rows)),
                  _const_spec((BUCKET_ROWS, LANES))],
        out_specs=[pl.BlockSpec((block_rows, ROW_WIDTH), row_map),
                   pl.BlockSpec((SUBLANES, block_rows), lambda b, t: (0, b * n_t + t)),
                   _const_spec((BUCKET_ROWS, LANES))],
        out_shape=[jax.ShapeDtypeStruct((S * T, ROW_WIDTH), F32),
                   jax.ShapeDtypeStruct((SUBLANES, S * T), jnp.int32),
                   jax.ShapeDtypeStruct((BUCKET_ROWS, LANES), F32)],
        scratch_shapes=[pltpu.VMEM((block_rows, D), _MXU_DTYPE), pltpu.VMEM((block_rows, D), _MXU_DTYPE),
                        pltpu.VMEM((BUCKET_ROWS, LANES), F32)],
        compiler_params=pltpu.CompilerParams(
            dimension_semantics=("arbitrary", "arbitrary"), vmem_limit_bytes=VMEM_LIMIT_BYTES),
        name="xattn",
    )(x.reshape(S * T, D), mem_k, mem_v, lw['w_xq'], lw['w_xo'], lw['ln_g'][1:2], lw['ln_b'][1:2],
      w_router_t, upper, cnt_in)


def _route_tables(cnt, n_tiles):
    cnt = cnt[:N_BUCKETS, 0].astype(jnp.int32)
    padded = ((cnt + EXPERT_TILE - 1) // EXPERT_TILE) * EXPERT_TILE
    ends = jnp.cumsum(padded)
    base = ends - padded
    tile_start = jnp.arange(n_tiles, dtype=jnp.int32) * EXPERT_TILE
    valid = tile_start < ends[-1]
    last_start = jnp.maximum(ends[-1] - EXPERT_TILE, 0)
    start = jnp.where(valid, tile_start, last_start)
    tb = jnp.sum((ends[None, :] <= start[:, None]).astype(jnp.int32), axis=1)
    tb = jnp.minimum(tb, N_BUCKETS - 1)
    group, pair = tb // len(PAIRS), tb % len(PAIRS)
    pair_a = jnp.asarray([a for a, _ in PAIRS], jnp.int32)
    pair_b = jnp.asarray([b for _, b in PAIRS], jnp.int32)
    ea = group * EXP_PER_GROUP + pair_a[pair]
    eb = group * EXP_PER_GROUP + pair_b[pair]
    used_tiles = (ends[-1:] // EXPERT_TILE).astype(jnp.int32)
    return base, base + cnt, padded - cnt, used_tiles, ea, eb, valid.astype(jnp.int32)


def _dispatch_kernel(pos_ref, pad_start_ref, pad_len_ref, used_tiles_ref, tok_p_ref, tok_s_ref,
                     xs_ref, zero_ref, sem, pad_sem, *, blocks_p):
    i = pl.program_id(0)
    rows = tok_p_ref.shape[0]
    n_tiles = xs_ref.shape[0] // EXPERT_TILE

    def unused_tile(t, wait):
        cp = pltpu.make_async_copy(zero_ref, xs_ref.at[pl.ds(t * EXPERT_TILE, EXPERT_TILE)], pad_sem)
        cp.wait() if wait else cp.start()

    def pad_copies(b, wait):
        off = pad_start_ref[b]
        left = pad_len_ref[b]
        head = jnp.minimum((SUBLANES - off % SUBLANES) % SUBLANES, left)
        for j in range(SUBLANES - 1):
            cp = pltpu.make_async_copy(zero_ref.at[pl.ds(0, 1)], xs_ref.at[pl.ds(off + j, 1)], pad_sem)

            @pl.when(j < head)
            def _():
                cp.wait() if wait else cp.start()
        off = off + head
        left = left - head
        size = ZERO_ROWS
        while size >= SUBLANES:
            dst = xs_ref.at[pl.ds(pl.multiple_of(off, SUBLANES), size)]
            cp = pltpu.make_async_copy(zero_ref.at[pl.ds(0, size)], dst, pad_sem)

            @pl.when(left >= size)
            def _():
                cp.wait() if wait else cp.start()
            take = left >= size
            off = jnp.where(take, off + size, off)
            left = jnp.where(take, left - size, left)
            size //= 2

    @pl.when(i == 0)
    def _():
        zero_ref[...] = jnp.zeros(zero_ref.shape, F32)
        for wait in (False, True):
            lax.fori_loop(used_tiles_ref[0], n_tiles, lambda t, c, w=wait: (unused_tile(t, w), c)[1], 0)
            lax.fori_loop(0, N_BUCKETS, lambda b, c, w=wait: (pad_copies(b, w), c)[1], 0)

    def scatter(tok_ref):
        def issue(k, carry):
            for j in range(DMA_UNROLL):
                r = k * DMA_UNROLL + j
                pltpu.make_async_copy(tok_ref.at[pl.ds(r, 1)], xs_ref.at[pl.ds(pos_ref[i * rows + r], 1)],
                                      sem).start()
            return carry
        lax.fori_loop(0, rows // DMA_UNROLL, issue, 0)
        pltpu.make_async_copy(tok_ref, xs_ref.at[pl.ds(0, rows)], sem).wait()

    @pl.when(i < blocks_p)
    def _():
        scatter(tok_p_ref)

    @pl.when(i >= blocks_p)
    def _():
        scatter(tok_s_ref)


def _dispatch(tok_p, tok_s, pos, pad_start, pad_len, used_tiles, n_rows):
    n_p, n_s = tok_p.shape[0], tok_s.shape[0]
    rows = min(PROMPT_BLOCK_ROWS, n_p, n_s)
    blocks_p, blocks_s = n_p // rows, n_s // rows
    block = (rows, ROW_WIDTH)
    return pl.pallas_call(
        functools.partial(_dispatch_kernel, blocks_p=blocks_p),
        grid_spec=pltpu.PrefetchScalarGridSpec(
            num_scalar_prefetch=4, grid=(blocks_p + blocks_s,),
            in_specs=[pl.BlockSpec(block, lambda i, *_: (jnp.minimum(i, blocks_p - 1), 0)),
                      pl.BlockSpec(block, lambda i, *_: (jnp.maximum(i - blocks_p, 0), 0))],
            out_specs=pl.BlockSpec(memory_space=pl.ANY),
            scratch_shapes=[pltpu.VMEM((EXPERT_TILE, ROW_WIDTH), F32),
                            pltpu.SemaphoreType.DMA, pltpu.SemaphoreType.DMA]),
        out_shape=jax.ShapeDtypeStruct((n_rows, ROW_WIDTH), F32),
        compiler_params=pltpu.CompilerParams(
            dimension_semantics=("arbitrary",), vmem_limit_bytes=VMEM_LIMIT_BYTES),
        name="dispatch",
    )(pos, pad_start, pad_len, used_tiles, tok_p, tok_s)


def _expert_kernel(ea_ref, eb_ref, valid_ref, xs_ref, wga_ref, wua_ref, wda_ref, wgb_ref, wub_ref, wdb_ref,
                   ys_ref, ga_ref, ua_ref, da_ref, gb_ref, ub_ref, db_ref):
    t = pl.program_id(0)

    @pl.when(valid_ref[t] > 0)
    def _():
        prev = jnp.maximum(t - 1, 0)

        @pl.when(jnp.logical_or(t == 0, ea_ref[t] != ea_ref[prev]))
        def _():
            ga_ref[...] = _mx(wga_ref[0])
            ua_ref[...] = _mx(wua_ref[0])
            da_ref[...] = _mx(wda_ref[0])

        @pl.when(jnp.logical_or(t == 0, eb_ref[t] != eb_ref[prev]))
        def _():
            gb_ref[...] = _mx(wgb_ref[0])
            ub_ref[...] = _mx(wub_ref[0])
            db_ref[...] = _mx(wdb_ref[0])

        x = _mx(xs_ref[:, 0:D_MODEL])
        cw = xs_ref[:, D_MODEL:ROW_WIDTH]

        gate_a = jnp.dot(x, ga_ref[...], preferred_element_type=F32)
        gate_b = jnp.dot(x, gb_ref[...], preferred_element_type=F32)
        up_a = jnp.dot(x, ua_ref[...], preferred_element_type=F32)
        up_b = jnp.dot(x, ub_ref[...], preferred_element_type=F32)
        h_a = _mx(gate_a * _sigmoid(gate_a) * up_a)
        h_b = _mx(gate_b * _sigmoid(gate_b) * up_b)
        y_a = jnp.dot(h_a, da_ref[...], preferred_element_type=F32)
        y_b = jnp.dot(h_b, db_ref[...], preferred_element_type=F32)
        y = cw[:, 0:1] * y_a + cw[:, 1:2] * y_b
        for s in range(FEATURE_ROWS):
            ys_ref[pl.ds(s, EXPERT_TILE, stride=FEATURE_ROWS), :] = y[:, s * LANES:(s + 1) * LANES]

    @pl.when(valid_ref[t] == 0)
    def _():
        ys_ref[...] = jnp.zeros(ys_ref.shape, F32)


def _experts(xs, ea, eb, valid, w_e_gate, w_e_up, w_e_down):
    n_tiles = ea.shape[0]
    D = D_MODEL
    wa = lambda t, ea, eb, v: (ea[t], 0, 0)
    wb = lambda t, ea, eb, v: (eb[t], 0, 0)
    gu = (1, D, D_EXPERT)
    dn = (1, D_EXPERT, D)
    return pl.pallas_call(
        _expert_kernel,
        grid_spec=pltpu.PrefetchScalarGridSpec(
            num_scalar_prefetch=3, grid=(n_tiles,),
            in_specs=[pl.BlockSpec((EXPERT_TILE, ROW_WIDTH), lambda t, ea, eb, v: (t * v[t], 0)),
                      pl.BlockSpec(gu, wa), pl.BlockSpec(gu, wa), pl.BlockSpec(dn, wa),
                      pl.BlockSpec(gu, wb), pl.BlockSpec(gu, wb), pl.BlockSpec(dn, wb)],
            out_specs=pl.BlockSpec((EXPERT_TILE * FEATURE_ROWS, LANES), lambda t, ea, eb, v: (t, 0)),
            scratch_shapes=[pltpu.VMEM((D, D_EXPERT), _MXU_DTYPE), pltpu.VMEM((D, D_EXPERT), _MXU_DTYPE),
                            pltpu.VMEM((D_EXPERT, D), _MXU_DTYPE), pltpu.VMEM((D, D_EXPERT), _MXU_DTYPE),
                            pltpu.VMEM((D, D_EXPERT), _MXU_DTYPE), pltpu.VMEM((D_EXPERT, D), _MXU_DTYPE)]),
        out_shape=jax.ShapeDtypeStruct((n_tiles * EXPERT_TILE * FEATURE_ROWS, LANES), F32),
        compiler_params=pltpu.CompilerParams(
            dimension_semantics=("arbitrary",), vmem_limit_bytes=VMEM_LIMIT_BYTES),
        name="experts",
    )(ea, eb, valid, xs, w_e_gate, w_e_up, w_e_down, w_e_gate, w_e_up, w_e_down)


def _combine_kernel(pos_ref, x_ref, ys_ref, lng_ref, lnb_ref, out_ref, buf0_ref, buf1_ref, sem, *, alpha):
    i = pl.program_id(0)
    n_steps = pl.num_programs(0)
    half = x_ref.shape[0] // 2
    bufs = (buf0_ref, buf1_ref)

    def fetch(block, slot):
        def issue(k, carry):
            for j in range(DMA_UNROLL):
                r = k * DMA_UNROLL + j
                src = pl.multiple_of(pos_ref[block * half + r] * FEATURE_ROWS, FEATURE_ROWS)
                pltpu.make_async_copy(ys_ref.at[pl.ds(src, FEATURE_ROWS)],
                                      bufs[slot].at[pl.ds(r * FEATURE_ROWS, FEATURE_ROWS)], sem.at[slot]).start()
            return carry
        lax.fori_loop(0, half // DMA_UNROLL, issue, 0)

    def finish(slot):
        buf = bufs[slot]
        pltpu.make_async_copy(ys_ref.at[pl.ds(0, half * FEATURE_ROWS)], buf, sem.at[slot]).wait()
        y = jnp.concatenate([buf[pl.ds(s, half, stride=FEATURE_ROWS), :] for s in range(FEATURE_ROWS)], axis=-1)
        rows = pl.ds(slot * half, half)
        out_ref[rows, :] = _layer_norm(alpha * x_ref[rows, :] + y, lng_ref[...], lnb_ref[...])

    @pl.when(i == 0)
    def _():
        fetch(0, 0)

    fetch(2 * i + 1, 1)
    finish(0)

    @pl.when(i + 1 < n_steps)
    def _():
        fetch(2 * i + 2, 0)

    finish(1)


def _combine(x2w, ys, pos, lw, alpha):
    n, D = x2w.shape[0], D_MODEL
    rows = min(2 * PROMPT_BLOCK_ROWS, n)
    row_map = lambda i, pos: (i, 0)
    half_buf = pltpu.VMEM((rows // 2 * FEATURE_ROWS, LANES), F32)
    return pl.pallas_call(
        functools.partial(_combine_kernel, alpha=alpha),
        grid_spec=pltpu.PrefetchScalarGridSpec(
            num_scalar_prefetch=1, grid=(n // rows,),
            in_specs=[pl.BlockSpec((rows, D), row_map), pl.BlockSpec(memory_space=pl.ANY),
                      pl.BlockSpec((1, D), lambda i, pos: (0, 0)), pl.BlockSpec((1, D), lambda i, pos: (0, 0))],
            out_specs=pl.BlockSpec((rows, D), row_map),
            scratch_shapes=[half_buf, half_buf, pltpu.SemaphoreType.DMA((2,))]),
        out_shape=jax.ShapeDtypeStruct((n, D), F32),
        compiler_params=pltpu.CompilerParams(
            dimension_semantics=("arbitrary",), vmem_limit_bytes=VMEM_LIMIT_BYTES),
        name="combine",
    )(pos, x2w, ys, lw['ln_g'][2:3], lw['ln_b'][2:3])


def _moe(x2_p, rb_p, x2_s, rb_s, cnt, experts, layer, lw, alpha):
    n_p, n_s = x2_p.shape[0], x2_s.shape[0]
    n_tiles = -(-(n_p + n_s + N_BUCKETS * (EXPERT_TILE - 1)) // EXPERT_TILE)
    base, pad_start, pad_len, used_tiles, ea, eb, valid = _route_tables(cnt, n_tiles)
    pos_p = base[rb_p[0]] + rb_p[1]
    pos_s = base[rb_s[0]] + rb_s[1]
    xs = _dispatch(x2_p, x2_s, jnp.concatenate([pos_p, pos_s]), pad_start, pad_len, used_tiles,
                   n_tiles * EXPERT_TILE)
    ys = _experts(xs, ea + layer * N_EXPERTS, eb + layer * N_EXPERTS, valid, *experts)
    return _combine(x2_p, ys, pos_p, lw, alpha), _combine(x2_s, ys, pos_s, lw, alpha)


def _memkv_kernel(mem_ref, wk_ref, wv_ref, k_ref, v_ref, kh_ref, vh_ref):
    mh = _mx(mem_ref[...])
    k = jnp.dot(mh, wk_ref[...], preferred_element_type=F32)
    v = jnp.dot(mh, wv_ref[...], preferred_element_type=F32)
    k_ref[...] = k
    v_ref[...] = v
    kh_ref[...] = k.astype(kh_ref.dtype)
    vh_ref[...] = v.astype(vh_ref.dtype)


def _mem_kv(mem, lw):
    n, D = mem.shape
    rows = min(PROMPT_BLOCK_ROWS, n)
    spec = pl.BlockSpec((rows, D), lambda i: (i, 0))
    return pl.pallas_call(
        _memkv_kernel,
        grid=(n // rows,),
        in_specs=[spec, _const_spec((D, D)), _const_spec((D, D))],
        out_specs=[spec, spec, spec, spec],
        out_shape=[jax.ShapeDtypeStruct((n, D), F32), jax.ShapeDtypeStruct((n, D), F32),
                   jax.ShapeDtypeStruct((n, D), _MXU_DTYPE), jax.ShapeDtypeStruct((n, D), _MXU_DTYPE)],
        compiler_params=pltpu.CompilerParams(
            dimension_semantics=("parallel",), vmem_limit_bytes=VMEM_LIMIT_BYTES),
        name="mem_kv",
    )(mem, lw['w_xk'], lw['w_xv'])


def _layer_weights(l, w_in, b_if, conv_w, gn_m, gn_r, w_out, w_xq, w_xk, w_xv, w_xo, ln_g, ln_b):
    wi = w_in[l]
    gates = wi[:, GATE_COL0:GATE_COL0 + N_GATES]
    b = b_if[l].astype(F32)
    return {
        'w_main': _mx(jnp.concatenate([wi[:, :GATE_COL0], wi[:, GATE_COL0 + N_GATES:]], axis=1)),
        'w_gc': _mx(jnp.pad(gates, ((0, 0), (0, LANES - N_GATES)))),
        'w_gr': _mx(gates.T),
        'b_col': jnp.pad(b, (0, LANES - N_GATES))[None, :],
        'b_row': b[:, None],
        'conv_w': conv_w[l].astype(F32),
        'gn': jnp.concatenate([gn_m[l], gn_r[l]]).astype(F32)[None, :],
        'w_out': _mx(w_out[l]), 'w_xq': _mx(w_xq[l]), 'w_xk': _mx(w_xk[l]), 'w_xv': _mx(w_xv[l]),
        'w_xo': _mx(w_xo[l]),
        'ln_g': ln_g[l].astype(F32), 'ln_b': ln_b[l].astype(F32),
    }


def kernel(x_prompt, x_sample, mem_prompt, cache_mem_k, cache_mem_v, state_conv, state_mlstm_C,
           state_mlstm_n, state_mlstm_m, state_ret_S, w_in, b_if, conv_w, gn_m, gn_r, w_out, w_xq,
           w_xk, w_xv, w_xo, w_router, w_e_gate, w_e_up, w_e_down, ln_g, ln_b):
    depth = w_in.shape[0]
    alpha = (2 * depth) ** 0.25
    Bp, Tp, D = x_prompt.shape
    Bs, Ts, _ = x_sample.shape
    pos_p = jnp.arange(Tp)
    pos_s = PAST_LEN + jnp.arange(Ts)
    w_router_t = w_router.astype(F32).T
    mem2d = mem_prompt.reshape(Bp * N_MEM, D)
    xp, xs = x_prompt, x_sample
    experts = tuple(w.reshape(depth * N_EXPERTS, *w.shape[2:]) for w in (w_e_gate, w_e_up, w_e_down))
    cnt0 = jnp.zeros((BUCKET_ROWS, LANES), F32)
    outs = [[] for _ in range(12)]
    for l in range(depth):
        lw = _layer_weights(l, w_in, b_if, conv_w, gn_m, gn_r, w_out, w_xq, w_xk, w_xv, w_xo, ln_g, ln_b)
        kp, vp, kph, vph = _mem_kv(mem2d, lw)
        x1, cb, c_n, n_n, m_n, s_n = _mixer(xp, lw, pos_p, None, alpha)
        x2_p, rb_p, cnt = _xattn(x1, kph.reshape(Bp, N_MEM, D), vph.reshape(Bp, N_MEM, D), lw,
                                        w_router_t, cnt0, alpha)
        for lst, val in zip(outs[:7], (kp.reshape(Bp, N_MEM, X_HEADS, X_DIM),
                                       vp.reshape(Bp, N_MEM, X_HEADS, X_DIM), cb, c_n, n_n, m_n, s_n)):
            lst.append(val)
        state = (state_conv[l], state_mlstm_C[l], state_mlstm_n[l],
                 jnp.broadcast_to(state_mlstm_m[l][:, :, None], (Bs, M_HEADS, LANES)), state_ret_S[l])
        x1, cb, c_n, n_n, m_n, s_n = _mixer(xs, lw, pos_s, state, alpha)
        x2_s, rb_s, cnt = _xattn(x1, _mx(cache_mem_k[l].reshape(Bs, N_MEM, D)),
                                        _mx(cache_mem_v[l].reshape(Bs, N_MEM, D)), lw, w_router_t, cnt, alpha)
        for lst, val in zip(outs[7:], (cb, c_n, n_n, m_n, s_n)):
            lst.append(val)
        xp2, xs2 = _moe(x2_p, rb_p, x2_s, rb_s, cnt, experts, l, lw, alpha)
        xp, xs = xp2.reshape(Bp, Tp, D), xs2.reshape(Bs, Ts, D)
    return (xp, xs) + tuple(jnp.stack(o) for o in outs)
```

```python
import functools

import jax
import jax.numpy as jnp
from jax import lax
from jax.experimental import pallas as pl
from jax.experimental.pallas import tpu as pltpu

F32 = jnp.float32
_MXU_DTYPE = jnp.bfloat16

D_MODEL = 1024
PAST_LEN = 4096
N_MEM = 256
M_WIDTH = D_MODEL // 2
M_HEADS = 4
M_DIM = M_WIDTH // M_HEADS
R_WIDTH = D_MODEL - M_WIDTH
R_HEADS = 4
R_DIM = R_WIDTH // R_HEADS
CONV_W = 4
ROPE_BASE = 10000.0
X_HEADS = 4
X_DIM = D_MODEL // X_HEADS
N_EXPERTS = 16
N_GROUPS = 4
EXP_PER_GROUP = N_EXPERTS // N_GROUPS
D_EXPERT = D_MODEL // 2
LN_EPS = 1e-5
GATE_COL0 = 2 * M_WIDTH + 2 * M_WIDTH
N_GATES = 2 * M_HEADS
MAIN_WIDTH = 4 * D_MODEL

LANES = 128
SUBLANES = 8
VMEM_LIMIT_BYTES = 56 * 1024 * 1024

PROMPT_BLOCK_ROWS = 512
PROMPT_CHUNK = 256
SAMPLE_SEQS_PER_BLOCK = 8

PAIRS = ((0, 1), (0, 2), (0, 3), (1, 2), (1, 3), (2, 3))
N_BUCKETS = N_GROUPS * len(PAIRS)
BUCKET_ROWS = 32
EXPERT_TILE = 256
ROW_WIDTH = D_MODEL + LANES
FEATURE_ROWS = D_MODEL // LANES
DMA_UNROLL = 8
HEAD_GROUPS = ((('m', 0), ('m', 1)), (('r', 0), ('r', 1)), (('m', 2), ('m', 3)), (('r', 2), ('r', 3)))
PROJ_PIECE = 512
PROJ_EVERY = 3
SAMPLE_SEQS_INTERLEAVED = 2
XATTN_SEQ_GROUP = 4
ZERO_ROWS = EXPERT_TILE // 2


def _mx(a):
    return a.astype(_MXU_DTYPE)


def _dot(a, b):
    return jnp.dot(_mx(a), _mx(b), preferred_element_type=F32)


def _dot_nt(a, b):
    return lax.dot_general(_mx(a), _mx(b), (((1,), (1,)), ((), ())), preferred_element_type=F32)


def _dot_tn(a, b):
    return lax.dot_general(_mx(a), _mx(b), (((0,), (0,)), ((), ())), preferred_element_type=F32)


def _dot_exact(a, b):
    return jnp.dot(a, b, preferred_element_type=F32, precision=lax.Precision.HIGHEST)


def _layer_norm(y, g, b):
    mu = jnp.mean(y, axis=-1, keepdims=True)
    yc = y - mu
    var = jnp.mean(yc * yc, axis=-1, keepdims=True)
    return yc * lax.rsqrt(var + LN_EPS) * g + b


def _head_norm(h):
    mu = jnp.mean(h, axis=-1, keepdims=True)
    hc = h - mu
    var = jnp.mean(hc * hc, axis=-1, keepdims=True)
    return hc * lax.rsqrt(var + LN_EPS)


def _sigmoid(x):
    return 1.0 / (1.0 + jnp.exp(-x))


def _log_sigmoid(x):
    return jnp.minimum(x, 0.0) - jnp.log1p(jnp.exp(-jnp.abs(x)))


def _run_interleaved(gens):
    live = list(gens)
    while live:
        for g in list(live):
            try:
                next(g)
            except StopIteration:
                live.remove(g)


def _spread(gen, n, every):
    for i in range(n * every):
        if i % every == 0:
            try:
                next(gen)
            except StopIteration:
                return
        yield


def _const_spec(shape):
    n = len(shape)
    return pl.BlockSpec(shape, lambda *_: (0,) * n)


def _mixer_kernel(*refs, chunk, n_chunks, carried, alpha):
    (x_ref, wmain_ref, wgc_ref, wgr_ref, bcol_ref, brow_ref, convw_ref, gn_ref, wout_ref,
     lng_ref, lnb_ref, cos_ref, sin_ref, dmask_ref, cd_ref, wkr_ref, gl_ref) = refs[:17]
    if carried:
        state_in = None
        xn_ref = refs[17]
        rest = refs[18:]
    else:
        state_in = refs[17:22]
        rest = refs[22:]
    x1_ref, conv_out, c_out, n_out, m_out, s_out = rest[:6]
    proj_ref, gcol_ref, hbuf_ref, ub_ref, xh_ref = rest[6:]
    if carried:
        conv_in = None
        c_src, n_src, m_src, s_src = c_out, n_out, m_out, s_out
    else:
        conv_in, c_src, n_src, m_src, s_src = state_in
    L = chunk

    xb = x_ref[...]
    xh = _mx(xb)
    xh_ref[...] = xh

    def project(xh_rows, buf, rows):
        for j in range(MAIN_WIDTH // PROJ_PIECE):
            cols = slice(j * PROJ_PIECE, (j + 1) * PROJ_PIECE)
            proj_ref[buf, rows, cols] = jnp.dot(xh_rows, wmain_ref[:, cols], preferred_element_type=F32)
            yield
        gcol_ref[buf, rows, :] = jnp.dot(xh_rows, wgc_ref[...], preferred_element_type=F32) + bcol_ref[...]
        yield

    if carried:
        cur = pl.program_id(1) % 2
        nxt = 1 - cur

        @pl.when(pl.program_id(1) == 0)
        def _():
            for _ in project(xh, 0, slice(None)):
                pass
            ub_ref[0:SUBLANES, :] = jnp.zeros((SUBLANES, D_MODEL), F32)
            c_out[...] = jnp.zeros(c_out.shape, F32)
            n_out[...] = jnp.zeros(n_out.shape, F32)
            m_out[...] = jnp.zeros(m_out.shape, F32)
            s_out[...] = jnp.zeros(s_out.shape, F32)
    else:
        cur = 0
        for _ in project(xh, 0, slice(None)):
            pass

    row_i = lax.broadcasted_iota(jnp.int32, (L, L), 0)
    col_i = lax.broadcasted_iota(jnp.int32, (L, L), 1)
    causal = row_i >= col_i
    tri = causal.astype(F32)
    tri_t = (row_i <= col_i).astype(F32)
    convw = convw_ref[...]
    gn = gn_ref[...]

    def chunk_heads(c):
        r0 = pl.multiple_of(c * L, L)
        rows = pl.ds(r0, L)
        si = 0 if carried else c
        tab_rows = rows if carried else pl.ds(0, L)

        u = proj_ref[cur, rows, 0:2 * M_WIDTH]
        if not carried:
            ub_ref[SUBLANES - 3:SUBLANES, :] = conv_in[si]
        ub_ref[SUBLANES:SUBLANES + L, :] = u
        acc = u * convw[CONV_W - 1:CONV_W, :]
        for j in range(CONV_W - 1):
            acc = acc + ub_ref[SUBLANES - 3 + j:SUBLANES - 3 + j + L, :] * convw[j:j + 1, :]
        last_rows = ub_ref[SUBLANES + L - 3:SUBLANES + L, :]
        conv_out[si] = last_rows
        if carried:
            ub_ref[SUBLANES - 3:SUBLANES, :] = last_rows
        qk = acc * _sigmoid(acc)

        gc = gcol_ref[cur, rows, :]
        gr = _dot_nt(wgr_ref[...], xh_ref[rows, :]) + brow_ref[...]
        b_cols = _dot_exact(tri, _log_sigmoid(gc))
        b_rows = _dot_exact(_log_sigmoid(gr), tri_t)

        cos = cos_ref[tab_rows, :]
        sin = sin_ref[tab_rows, :]

        def mlstm_head(h):
            lo, hi = h * M_DIM, (h + 1) * M_DIM
            q = qk[:, lo:hi] * (M_DIM ** -0.5)
            k = qk[:, M_WIDTH + lo:M_WIDTH + hi]
            v = proj_ref[cur, rows, 2 * M_WIDTH + lo:2 * M_WIDTH + hi]
            c0 = c_src[si, h]
            n0 = n_src[si, h:h + 1, :]
            m0 = m_src[si, h:h + 1, 0:1]
            b_c = b_cols[:, M_HEADS + h:M_HEADS + h + 1]
            b_r = b_rows[M_HEADS + h:M_HEADS + h + 1, :]
            ig_c = gc[:, h:h + 1]
            ig_r = gr[h:h + 1, :]
            a_c = b_c + m0
            s_raw = _dot_nt(q, k)
            yield
            qc = _dot(q, c0)
            yield
            dmat = jnp.where(causal, b_c - b_r + ig_r, -jnp.inf)
            yield
            m_c = jnp.maximum(a_c, jnp.max(dmat, axis=-1, keepdims=True))
            yield
            w_intra = jnp.exp(dmat - m_c)
            yield
            w_inter = jnp.exp(a_c - m_c)
            s = s_raw * w_intra
            yield
            sv = _dot(s, v)
            yield
            m_last = m_c[L - 1:L, :]
            wk_c = jnp.exp(b_c[L - 1:L, :] - b_c + ig_c - m_last)
            decay = jnp.exp(a_c[L - 1:L, :] - m_last)
            kw = wk_c * k
            upd = _dot_tn(kw, v)
            yield
            num = sv + w_inter * qc
            den = jnp.sum(s, axis=-1, keepdims=True) + w_inter * jnp.sum(q * n0, axis=-1, keepdims=True)
            yield
            hm = num / jnp.maximum(jnp.abs(den), jnp.exp(-m_c))
            yield
            c_out[si, h] = decay * c0 + upd
            n_out[si, h:h + 1, :] = decay * n0 + jnp.sum(kw, axis=0, keepdims=True)
            m_out[si, h:h + 1, :] = jnp.broadcast_to(m_last, (1, LANES))
            yield
            og = proj_ref[cur, rows, 3 * M_WIDTH + lo:3 * M_WIDTH + hi]
            hm = _head_norm(hm) * gn[:, lo:hi] * _sigmoid(og)
            hbuf_ref[rows, lo:hi] = hm.astype(hbuf_ref.dtype)
            yield

        def retention_head(h):
            lo, hi = h * R_DIM, (h + 1) * R_DIM
            base = 4 * M_WIDTH
            rq = proj_ref[cur, rows, base + lo:base + hi]
            rk = proj_ref[cur, rows, base + R_WIDTH + lo:base + R_WIDTH + hi]
            v = proj_ref[cur, rows, base + 2 * R_WIDTH + lo:base + 2 * R_WIDTH + hi]
            q = rq * cos + pltpu.roll(rq, R_DIM // 2, 1) * sin
            yield
            k = (rk * cos + pltpu.roll(rk, R_DIM // 2, 1) * sin) * (R_DIM ** -0.5)
            s0 = s_src[si, h]
            yield
            s_raw = _dot_nt(q, k)
            yield
            qs = _dot(q, s0)
            yield
            upd = _dot_tn(wkr_ref[h] * k, v)
            yield
            s = s_raw * dmask_ref[h]
            yield
            sv = _dot(s, v)
            yield
            hr = sv + qs * cd_ref[h]
            s_out[si, h] = gl_ref[h] * s0 + upd
            yield
            rg = proj_ref[cur, rows, base + 3 * R_WIDTH + lo:base + 3 * R_WIDTH + hi]
            hr = _head_norm(hr) * gn[:, M_WIDTH + lo:M_WIDTH + hi] * (rg * _sigmoid(rg))
            hbuf_ref[rows, M_WIDTH + lo:M_WIDTH + hi] = hr.astype(hbuf_ref.dtype)
            yield

        return [[(mlstm_head if kind == 'm' else retention_head)(h) for kind, h in group]
                for group in HEAD_GROUPS]

    seq_par = 1 if carried else SAMPLE_SEQS_INTERLEAVED

    def chunk_body(i, carry):
        per_chunk = [chunk_heads(i * seq_par + j) for j in range(seq_par)]
        next_proj = None
        if carried:
            rows_n = pl.ds(pl.multiple_of(i * L, L), L)
            next_proj = project(_mx(xn_ref[rows_n, :]), nxt, rows_n)
        n_groups = len(per_chunk[0])
        for gi, groups in enumerate(zip(*per_chunk)):
            gens = [g for group in groups for g in group]
            if next_proj is not None:
                n_stage = -(-(MAIN_WIDTH // PROJ_PIECE + 1) // n_groups)
                gens.append(_spread(next_proj, n_stage if gi < n_groups - 1 else 2 * n_stage, PROJ_EVERY))
            _run_interleaved(gens)
        return carry

    lax.fori_loop(0, n_chunks // seq_par, chunk_body, 0)

    mix = jnp.dot(hbuf_ref[...], wout_ref[...], preferred_element_type=F32)
    x1_ref[...] = _layer_norm(alpha * xb + mix, lng_ref[...], lnb_ref[...])


def _retention_tables(L):
    log_g = jnp.log(1.0 - 2.0 ** (-5.0 - jnp.arange(R_HEADS, dtype=F32)))
    idx = jnp.arange(L, dtype=F32)
    rel = idx[:, None] - idx[None, :]
    dmask = jnp.where(rel >= 0, jnp.exp(log_g[:, None, None] * jnp.maximum(rel, 0.0)), 0.0)
    cd = jnp.exp(log_g[:, None] * (idx + 1.0))
    wk = jnp.exp(log_g[:, None] * (L - 1.0 - idx))
    gl = jnp.exp(log_g * L)
    bc = lambda t: jnp.broadcast_to(t[:, :, None], (R_HEADS, L, LANES))
    return dmask, bc(cd), bc(wk), jnp.broadcast_to(gl[:, None, None], (R_HEADS, 1, LANES))


def _rotary_tables(pos):
    half = R_DIM // 2
    inv = ROPE_BASE ** (-jnp.arange(half, dtype=F32) / half)
    ang = pos.astype(F32)[:, None] * inv[None, :]
    cos, sin = jnp.cos(ang), jnp.sin(ang)
    return jnp.concatenate([cos, cos], axis=-1), jnp.concatenate([-sin, sin], axis=-1)


def _mixer(x, lw, pos, state, alpha):
    S, T, D = x.shape
    carried = state is None
    if carried:
        block_rows = min(PROMPT_BLOCK_ROWS, T)
        L = min(PROMPT_CHUNK, block_rows)
        seqs = 1
        n_t = T // block_rows
        grid = (S, n_t)
    else:
        seqs = min(SAMPLE_SEQS_PER_BLOCK, S)
        L = T
        block_rows = seqs * T
        n_t = 1
        grid = (S // seqs, 1)
    n_chunks = block_rows // L
    xr = x.reshape(S * T, D)
    cos, sin = _rotary_tables(pos)
    dmask, cd, wkr, gl = _retention_tables(L)

    row_spec = pl.BlockSpec((block_rows, D), lambda b, t: (b * n_t + t, 0))
    tab_spec = pl.BlockSpec((block_rows if carried else T, LANES), lambda b, t: (t, 0))
    in_specs = [
        row_spec,
        _const_spec((D, MAIN_WIDTH)), _const_spec((D, LANES)), _const_spec((N_GATES, D)),
        _const_spec((1, LANES)), _const_spec((N_GATES, 1)), _const_spec((CONV_W, 2 * M_WIDTH)),
        _const_spec((1, D)), _const_spec((D, D)), _const_spec((1, D)), _const_spec((1, D)),
        tab_spec, tab_spec,
        _const_spec((R_HEADS, L, L)), _const_spec((R_HEADS, L, LANES)), _const_spec((R_HEADS, L, LANES)),
        _const_spec((R_HEADS, 1, LANES)),
    ]
    args = [xr, lw['w_main'], lw['w_gc'], lw['w_gr'], lw['b_col'], lw['b_row'], lw['conv_w'],
            lw['gn'], lw['w_out'], lw['ln_g'][0:1], lw['ln_b'][0:1], cos, sin, dmask, cd, wkr, gl]
    state_specs = [
        pl.BlockSpec((seqs, CONV_W - 1, 2 * M_WIDTH), lambda b, t: (b, 0, 0)),
        pl.BlockSpec((seqs, M_HEADS, M_DIM, M_DIM), lambda b, t: (b, 0, 0, 0)),
        pl.BlockSpec((seqs, M_HEADS, M_DIM), lambda b, t: (b, 0, 0)),
        pl.BlockSpec((seqs, M_HEADS, LANES), lambda b, t: (b, 0, 0)),
        pl.BlockSpec((seqs, R_HEADS, R_DIM, R_DIM), lambda b, t: (b, 0, 0, 0)),
    ]
    if carried:
        in_specs.append(pl.BlockSpec((block_rows, D), lambda b, t: (b * n_t + jnp.minimum(t + 1, n_t - 1), 0)))
        args.append(xr)
    else:
        in_specs += state_specs
        args += list(state)
    out_shape = [
        jax.ShapeDtypeStruct((S * T, D), F32),
        jax.ShapeDtypeStruct((S, CONV_W - 1, 2 * M_WIDTH), F32),
        jax.ShapeDtypeStruct((S, M_HEADS, M_DIM, M_DIM), F32),
        jax.ShapeDtypeStruct((S, M_HEADS, M_DIM), F32),
        jax.ShapeDtypeStruct((S, M_HEADS, LANES), F32),
        jax.ShapeDtypeStruct((S, R_HEADS, R_DIM, R_DIM), F32),
    ]
    out_specs = [row_spec] + state_specs
    scratch = [
        pltpu.VMEM((2 if carried else 1, block_rows, MAIN_WIDTH), F32),
        pltpu.VMEM((2 if carried else 1, block_rows, LANES), F32),
        pltpu.VMEM((block_rows, D), _MXU_DTYPE),
        pltpu.VMEM((SUBLANES + L, 2 * M_WIDTH), F32),
        pltpu.VMEM((block_rows, D), _MXU_DTYPE),
    ]
    outs = pl.pallas_call(
        functools.partial(_mixer_kernel, chunk=L, n_chunks=n_chunks, carried=carried, alpha=alpha),
        grid=grid, in_specs=in_specs, out_specs=out_specs, out_shape=out_shape,
        scratch_shapes=scratch,
        compiler_params=pltpu.CompilerParams(
            dimension_semantics=("parallel", "arbitrary"), vmem_limit_bytes=VMEM_LIMIT_BYTES),
        name="mixer_prompt" if carried else "mixer_sample",
    )(*args)
    x1, conv_new, c_new, n_new, m_new, s_new = outs
    return x1.reshape(S, T, D), conv_new, c_new, n_new, m_new[:, :, 0], s_new


def _route_rows(logits_t):
    mx = jnp.max(logits_t, axis=0, keepdims=True)
    ex = jnp.exp(logits_t - mx)
    probs = ex / jnp.sum(ex, axis=0, keepdims=True)
    p = [probs[e:e + 1, :] for e in range(N_EXPERTS)]

    def first_max(vals):
        m = vals[0]
        for t in vals[1:]:
            m = jnp.maximum(m, t)
        taken = None
        sel = []
        for t in vals:
            is_max = t == m
            if taken is None:
                pick, taken = is_max, is_max
            else:
                pick = jnp.logical_and(is_max, jnp.logical_not(taken))
                taken = jnp.logical_or(taken, is_max)
            sel.append(pick)
        return m, sel

    scores, picked, denom = [], [], []
    for g in range(N_GROUPS):
        vals = p[g * EXP_PER_GROUP:(g + 1) * EXP_PER_GROUP]
        m1, sel1 = first_max(vals)
        rest = [jnp.where(s1, -1.0, t) for s1, t in zip(sel1, vals)]
        m2, sel2 = first_max(rest)
        scores.append(m1 + m2)
        denom.append(m1 + m2)
        picked.append([jnp.logical_or(a, b) for a, b in zip(sel1, sel2)])
    _, gsel = first_max(scores)
    bucket = jnp.zeros(p[0].shape, jnp.int32)
    ca = jnp.zeros(p[0].shape, F32)
    cb = jnp.zeros(p[0].shape, F32)
    for g in range(N_GROUPS):
        w = [p[g * EXP_PER_GROUP + j] / denom[g] for j in range(EXP_PER_GROUP)]
        for pi, (a, b) in enumerate(PAIRS):
            hit = jnp.logical_and(gsel[g], jnp.logical_and(picked[g][a], picked[g][b]))
            bucket = jnp.where(hit, g * len(PAIRS) + pi, bucket)
            ca = jnp.where(hit, w[a], ca)
            cb = jnp.where(hit, w[b], cb)
    return bucket, ca, cb


def _xattn_kernel(x_ref, k_ref, v_ref, wq_ref, wo_ref, lng_ref, lnb_ref, wr_ref, upper_ref, cnt_in_ref,
                  x2w_ref, rb_ref, cnt_out_ref, q_ref, o_ref, run_ref, *, seqs, seq_rows, alpha):
    @pl.when(jnp.logical_and(pl.program_id(0) == 0, pl.program_id(1) == 0))
    def _():
        run_ref[...] = cnt_in_ref[...]

    xb = x_ref[...]
    q_ref[...] = _dot(xb, wq_ref[...]).astype(q_ref.dtype)
    scale = X_DIM ** -0.5
    def attend(sq, h):
        r0 = sq * seq_rows
        lo, hi = h * X_DIM, (h + 1) * X_DIM
        s = _dot_nt(q_ref[r0:r0 + seq_rows, lo:hi], k_ref[sq, :, lo:hi]) * scale
        yield
        e = jnp.exp(s - jnp.max(s, axis=-1, keepdims=True))
        yield
        p = e / jnp.sum(e, axis=-1, keepdims=True)
        yield
        o_ref[r0:r0 + seq_rows, lo:hi] = _dot(p, v_ref[sq, :, lo:hi]).astype(o_ref.dtype)

    for sq0 in range(0, seqs, XATTN_SEQ_GROUP):
        _run_interleaved([attend(sq, h) for sq in range(sq0, min(sq0 + XATTN_SEQ_GROUP, seqs))
                          for h in range(X_HEADS)])
    att = jnp.dot(o_ref[...], wo_ref[...], preferred_element_type=F32)
    x2 = _layer_norm(alpha * xb + att, lng_ref[...], lnb_ref[...])
    x2w_ref[:, 0:D_MODEL] = x2
    n = xb.shape[0]

    logits_t = lax.dot_general(wr_ref[...], x2, (((1,), (1,)), ((), ())),
                               preferred_element_type=F32, precision=lax.Precision.HIGHEST)
    bucket, ca, cb = _route_rows(logits_t)

    onehot = (lax.broadcasted_iota(jnp.int32, (BUCKET_ROWS, n), 0) == bucket).astype(F32)
    earlier = jnp.dot(_mx(onehot), upper_ref[...], preferred_element_type=F32)
    run = run_ref[...]
    rank = jnp.sum(onehot * (earlier + run[:, 0:1]), axis=0, keepdims=True)
    run_ref[...] = run + jnp.sum(onehot, axis=1, keepdims=True)
    cnt_out_ref[...] = run_ref[...]
    rb_ref[...] = jnp.concatenate(
        [bucket, rank.astype(jnp.int32), jnp.zeros((SUBLANES - 2, n), jnp.int32)], axis=0)

    cw_t = jnp.concatenate([ca, cb, jnp.zeros((LANES - 2, n), F32)], axis=0)
    x2w_ref[:, D_MODEL:ROW_WIDTH] = cw_t.T


def _xattn(x, mem_k, mem_v, lw, w_router_t, cnt_in, alpha):
    S, T, D = x.shape
    if T >= PROMPT_BLOCK_ROWS:
        seqs, seq_rows = 1, PROMPT_BLOCK_ROWS
    else:
        seqs, seq_rows = min(SAMPLE_SEQS_PER_BLOCK, S), T
    n_t = T // seq_rows
    block_rows = seqs * seq_rows
    grid = (S // seqs, n_t)
    row_map = lambda b, t: (b * n_t + t, 0)
    kv_spec = pl.BlockSpec((seqs, N_MEM, D), lambda b, t: (b, 0, 0))
    idx = jnp.arange(block_rows)
    upper = _mx(idx[:, None] < idx[None, :])
    return pl.pallas_call(
        functools.partial(_xattn_kernel, seqs=seqs, seq_rows=seq_rows, alpha=alpha),
        grid=grid,
        in_specs=[pl.BlockSpec((block_rows, D), row_map), kv_spec, kv_spec,
                  _const_spec((D, D)), _const_spec((D, D)), _const_spec((1, D)), _const_spec((1, D)),
                  _const_spec((N_EXPERTS, D)), _const_spec((block_rows, block_rows)),
                  _const_spec((BUCKET_ROWS, LANES))],
        out_specs=[pl.BlockSpec((block_rows, ROW_WIDTH), row_map),
                   pl.BlockSpec((SUBLANES, block_rows), lambda b, t: (0, b * n_t + t)),
                   _const_spec((BUCKET_ROWS, LANES))],
        out_shape=[jax.ShapeDtypeStruct((S * T, ROW_WIDTH), F32),
                   jax.ShapeDtypeStruct((SUBLANES, S * T), jnp.int32),
                   jax.ShapeDtypeStruct((BUCKET_ROWS, LANES), F32)],
        scratch_shapes=[pltpu.VMEM((block_rows, D), _MXU_DTYPE), pltpu.VMEM((block_rows, D), _MXU_DTYPE),
                        pltpu.VMEM((BUCKET_ROWS, LANES), F32)],
        compiler_params=pltpu.CompilerParams(
            dimension_semantics=("arbitrary", "arbitrary"), vmem_limit_bytes=VMEM_LIMIT_BYTES),
        name="xattn",
    )(x.reshape(S * T, D), mem_k, mem_v, lw['w_xq'], lw['w_xo'], lw['ln_g'][1:2], lw['ln_b'][1:2],
      w_router_t, upper, cnt_in)


def _route_tables(cnt, n_tiles):
    cnt = cnt[:N_BUCKETS, 0].astype(jnp.int32)
    padded = ((cnt + EXPERT_TILE - 1) // EXPERT_TILE) * EXPERT_TILE
    ends = jnp.cumsum(padded)
    base = ends - padded
    tile_start = jnp.arange(n_tiles, dtype=jnp.int32) * EXPERT_TILE
    valid = tile_start < ends[-1]
    last_start = jnp.maximum(ends[-1] - EXPERT_TILE, 0)
    start = jnp.where(valid, tile_start, last_start)
    tb = jnp.sum((ends[None, :] <= start[:, None]).astype(jnp.int32), axis=1)
    tb = jnp.minimum(tb, N_BUCKETS - 1)
    group, pair = tb // len(PAIRS), tb % len(PAIRS)
    pair_a = jnp.asarray([a for a, _ in PAIRS], jnp.int32)
    pair_b = jnp.asarray([b for _, b in PAIRS], jnp.int32)
    ea = group * EXP_PER_GROUP + pair_a[pair]
    eb = group * EXP_PER_GROUP + pair_b[pair]
    used_tiles = (ends[-1:] // EXPERT_TILE).astype(jnp.int32)
    return base, base + cnt, padded - cnt, used_tiles, ea, eb, valid.astype(jnp.int32)


def _dispatch_kernel(pos_ref, pad_start_ref, pad_len_ref, used_tiles_ref, tok_p_ref, tok_s_ref,
                     xs_ref, zero_ref, sem, pad_sem, *, blocks_p):
    i = pl.program_id(0)
    rows = tok_p_ref.shape[0]
    n_tiles = xs_ref.shape[0] // EXPERT_TILE

    def unused_tile(t, wait):
        cp = pltpu.make_async_copy(zero_ref, xs_ref.at[pl.ds(t * EXPERT_TILE, EXPERT_TILE)], pad_sem)
        cp.wait() if wait else cp.start()

    def pad_copies(b, wait):
        off = pad_start_ref[b]
        left = pad_len_ref[b]
        head = jnp.minimum((SUBLANES - off % SUBLANES) % SUBLANES, left)
        for j in range(SUBLANES - 1):
            cp = pltpu.make_async_copy(zero_ref.at[pl.ds(0, 1)], xs_ref.at[pl.ds(off + j, 1)], pad_sem)

            @pl.when(j < head)
            def _():
                cp.wait() if wait else cp.start()
        off = off + head
        left = left - head
        size = ZERO_ROWS
        while size >= SUBLANES:
            dst = xs_ref.at[pl.ds(pl.multiple_of(off, SUBLANES), size)]
            cp = pltpu.make_async_copy(zero_ref.at[pl.ds(0, size)], dst, pad_sem)

            @pl.when(left >= size)
            def _():
                cp.wait() if wait else cp.start()
            take = left >= size
            off = jnp.where(take, off + size, off)
            left = jnp.where(take, left - size, left)
            size //= 2

    @pl.when(i == 0)
    def _():
        zero_ref[...] = jnp.zeros(zero_ref.shape, F32)
        for wait in (False, True):
            lax.fori_loop(used_tiles_ref[0], n_tiles, lambda t, c, w=wait: (unused_tile(t, w), c)[1], 0)
            lax.fori_loop(0, N_BUCKETS, lambda b, c, w=wait: (pad_copies(b, w), c)[1], 0)

    def scatter(tok_ref):
        def issue(k, carry):
            for j in range(DMA_UNROLL):
                r = k * DMA_UNROLL + j
                pltpu.make_async_copy(tok_ref.at[pl.ds(r, 1)], xs_ref.at[pl.ds(pos_ref[i * rows + r], 1)],
                                      sem).start()
            return carry
        lax.fori_loop(0, rows // DMA_UNROLL, issue, 0)
        pltpu.make_async_copy(tok_ref, xs_ref.at[pl.ds(0, rows)], sem).wait()

    @pl.when(i < blocks_p)
    def _():
        scatter(tok_p_ref)

    @pl.when(i >= blocks_p)
    def _():
        scatter(tok_s_ref)


def _dispatch(tok_p, tok_s, pos, pad_start, pad_len, used_tiles, n_rows):
    n_p, n_s = tok_p.shape[0], tok_s.shape[0]
    rows = min(PROMPT_BLOCK_ROWS, n_p, n_s)
    blocks_p, blocks_s = n_p // rows, n_s // rows
    block = (rows, ROW_WIDTH)
    return pl.pallas_call(
        functools.partial(_dispatch_kernel, blocks_p=blocks_p),
        grid_spec=pltpu.PrefetchScalarGridSpec(
            num_scalar_prefetch=4, grid=(blocks_p + blocks_s,),
            in_specs=[pl.BlockSpec(block, lambda i, *_: (jnp.minimum(i, blocks_p - 1), 0)),
                      pl.BlockSpec(block, lambda i, *_: (jnp.maximum(i - blocks_p, 0), 0))],
            out_specs=pl.BlockSpec(memory_space=pl.ANY),
            scratch_shapes=[pltpu.VMEM((EXPERT_TILE, ROW_WIDTH), F32),
                            pltpu.SemaphoreType.DMA, pltpu.SemaphoreType.DMA]),
        out_shape=jax.ShapeDtypeStruct((n_rows, ROW_WIDTH), F32),
        compiler_params=pltpu.CompilerParams(
            dimension_semantics=("arbitrary",), vmem_limit_bytes=VMEM_LIMIT_BYTES),
        name="dispatch",
    )(pos, pad_start, pad_len, used_tiles, tok_p, tok_s)


def _expert_kernel(ea_ref, eb_ref, valid_ref, xs_ref, wga_ref, wua_ref, wda_ref, wgb_ref, wub_ref, wdb_ref,
                   ys_ref, ga_ref, ua_ref, da_ref, gb_ref, ub_ref, db_ref):
    t = pl.program_id(0)

    @pl.when(valid_ref[t] > 0)
    def _():
        prev = jnp.maximum(t - 1, 0)

        @pl.when(jnp.logical_or(t == 0, ea_ref[t] != ea_ref[prev]))
        def _():
            ga_ref[...] = _mx(wga_ref[0])
            ua_ref[...] = _mx(wua_ref[0])
            da_ref[...] = _mx(wda_ref[0])

        @pl.when(jnp.logical_or(t == 0, eb_ref[t] != eb_ref[prev]))
        def _():
            gb_ref[...] = _mx(wgb_ref[0])
            ub_ref[...] = _mx(wub_ref[0])
            db_ref[...] = _mx(wdb_ref[0])

        x = _mx(xs_ref[:, 0:D_MODEL])
        cw = xs_ref[:, D_MODEL:ROW_WIDTH]

        gate_a = jnp.dot(x, ga_ref[...], preferred_element_type=F32)
        gate_b = jnp.dot(x, gb_ref[...], preferred_element_type=F32)
        up_a = jnp.dot(x, ua_ref[...], preferred_element_type=F32)
        up_b = jnp.dot(x, ub_ref[...], preferred_element_type=F32)
        h_a = _mx(gate_a * _sigmoid(gate_a) * up_a)
        h_b = _mx(gate_b * _sigmoid(gate_b) * up_b)
        y_a = jnp.dot(h_a, da_ref[...], preferred_element_type=F32)
        y_b = jnp.dot(h_b, db_ref[...], preferred_element_type=F32)
        y = cw[:, 0:1] * y_a + cw[:, 1:2] * y_b
        for s in range(FEATURE_ROWS):
            ys_ref[pl.ds(s, EXPERT_TILE, stride=FEATURE_ROWS), :] = y[:, s * LANES:(s + 1) * LANES]

    @pl.when(valid_ref[t] == 0)
    def _():
        ys_ref[...] = jnp.zeros(ys_ref.shape, F32)


def _experts(xs, ea, eb, valid, w_e_gate, w_e_up, w_e_down):
    n_tiles = ea.shape[0]
    D = D_MODEL
    wa = lambda t, ea, eb, v: (ea[t], 0, 0)
    wb = lambda t, ea, eb, v: (eb[t], 0, 0)
    gu = (1, D, D_EXPERT)
    dn = (1, D_EXPERT, D)
    return pl.pallas_call(
        _expert_kernel,
        grid_spec=pltpu.PrefetchScalarGridSpec(
            num_scalar_prefetch=3, grid=(n_tiles,),
            in_specs=[pl.BlockSpec((EXPERT_TILE, ROW_WIDTH), lambda t, ea, eb, v: (t * v[t], 0)),
                      pl.BlockSpec(gu, wa), pl.BlockSpec(gu, wa), pl.BlockSpec(dn, wa),
                      pl.BlockSpec(gu, wb), pl.BlockSpec(gu, wb), pl.BlockSpec(dn, wb)],
            out_specs=pl.BlockSpec((EXPERT_TILE * FEATURE_ROWS, LANES), lambda t, ea, eb, v: (t, 0)),
            scratch_shapes=[pltpu.VMEM((D, D_EXPERT), _MXU_DTYPE), pltpu.VMEM((D, D_EXPERT), _MXU_DTYPE),
                            pltpu.VMEM((D_EXPERT, D), _MXU_DTYPE), pltpu.VMEM((D, D_EXPERT), _MXU_DTYPE),
                            pltpu.VMEM((D, D_EXPERT), _MXU_DTYPE), pltpu.VMEM((D_EXPERT, D), _MXU_DTYPE)]),
        out_shape=jax.ShapeDtypeStruct((n_tiles * EXPERT_TILE * FEATURE_ROWS, LANES), F32),
        compiler_params=pltpu.CompilerParams(
            dimension_semantics=("arbitrary",), vmem_limit_bytes=VMEM_LIMIT_BYTES),
        name="experts",
    )(ea, eb, valid, xs, w_e_gate, w_e_up, w_e_down, w_e_gate, w_e_up, w_e_down)


def _combine_kernel(pos_ref, x_ref, ys_ref, lng_ref, lnb_ref, out_ref, buf0_ref, buf1_ref, sem, *, alpha):
    i = pl.program_id(0)
    n_steps = pl.num_programs(0)
    half = x_ref.shape[0] // 2
    bufs = (buf0_ref, buf1_ref)

    def fetch(block, slot):
        def issue(k, carry):
            for j in range(DMA_UNROLL):
                r = k * DMA_UNROLL + j
                src = pl.multiple_of(pos_ref[block * half + r] * FEATURE_ROWS, FEATURE_ROWS)
                pltpu.make_async_copy(ys_ref.at[pl.ds(src, FEATURE_ROWS)],
                                      bufs[slot].at[pl.ds(r * FEATURE_ROWS, FEATURE_ROWS)], sem.at[slot]).start()
            return carry
        lax.fori_loop(0, half // DMA_UNROLL, issue, 0)

    def finish(slot):
        buf = bufs[slot]
        pltpu.make_async_copy(ys_ref.at[pl.ds(0, half * FEATURE_ROWS)], buf, sem.at[slot]).wait()
        y = jnp.concatenate([buf[pl.ds(s, half, stride=FEATURE_ROWS), :] for s in range(FEATURE_ROWS)], axis=-1)
        rows = pl.ds(slot * half, half)
        out_ref[rows, :] = _layer_norm(alpha * x_ref[rows, :] + y, lng_ref[...], lnb_ref[...])

    @pl.when(i == 0)
    def _():
        fetch(0, 0)

    fetch(2 * i + 1, 1)
    finish(0)

    @pl.when(i + 1 < n_steps)
    def _():
        fetch(2 * i + 2, 0)

    finish(1)


def _combine(x2w, ys, pos, lw, alpha):
    n, D = x2w.shape[0], D_MODEL
    rows = min(2 * PROMPT_BLOCK_ROWS, n)
    row_map = lambda i, pos: (i, 0)
    half_buf = pltpu.VMEM((rows // 2 * FEATURE_ROWS, LANES), F32)
    return pl.pallas_call(
        functools.partial(_combine_kernel, alpha=alpha),
        grid_spec=pltpu.PrefetchScalarGridSpec(
            num_scalar_prefetch=1, grid=(n // rows,),
            in_specs=[pl.BlockSpec((rows, D), row_map), pl.BlockSpec(memory_space=pl.ANY),
                      pl.BlockSpec((1, D), lambda i, pos: (0, 0)), pl.BlockSpec((1, D), lambda i, pos: (0, 0))],
            out_specs=pl.BlockSpec((rows, D), row_map),
            scratch_shapes=[half_buf, half_buf, pltpu.SemaphoreType.DMA((2,))]),
        out_shape=jax.ShapeDtypeStruct((n, D), F32),
        compiler_params=pltpu.CompilerParams(
            dimension_semantics=("arbitrary",), vmem_limit_bytes=VMEM_LIMIT_BYTES),
        name="combine",
    )(pos, x2w, ys, lw['ln_g'][2:3], lw['ln_b'][2:3])


def _moe(x2_p, rb_p, x2_s, rb_s, cnt, experts, layer, lw, alpha):
    n_p, n_s = x2_p.shape[0], x2_s.shape[0]
    n_tiles = -(-(n_p + n_s + N_BUCKETS * (EXPERT_TILE - 1)) // EXPERT_TILE)
    base, pad_start, pad_len, used_tiles, ea, eb, valid = _route_tables(cnt, n_tiles)
    pos_p = base[rb_p[0]] + rb_p[1]
    pos_s = base[rb_s[0]] + rb_s[1]
    xs = _dispatch(x2_p, x2_s, jnp.concatenate([pos_p, pos_s]), pad_start, pad_len, used_tiles,
                   n_tiles * EXPERT_TILE)
    ys = _experts(xs, ea + layer * N_EXPERTS, eb + layer * N_EXPERTS, valid, *experts)
    return _combine(x2_p, ys, pos_p, lw, alpha), _combine(x2_s, ys, pos_s, lw, alpha)


def _memkv_kernel(mem_ref, wk_ref, wv_ref, k_ref, v_ref, kh_ref, vh_ref):
    mh = _mx(mem_ref[...])
    k = jnp.dot(mh, wk_ref[...], preferred_element_type=F32)
    v = jnp.dot(mh, wv_ref[...], preferred_element_type=F32)
    k_ref[...] = k
    v_ref[...] = v
    kh_ref[...] = k.astype(kh_ref.dtype)
    vh_ref[...] = v.astype(vh_ref.dtype)


def _mem_kv(mem, lw):
    n, D = mem.shape
    rows = min(PROMPT_BLOCK_ROWS, n)
    spec = pl.BlockSpec((rows, D), lambda i: (i, 0))
    return pl.pallas_call(
        _memkv_kernel,
        grid=(n // rows,),
        in_specs=[spec, _const_spec((D, D)), _const_spec((D, D))],
        out_specs=[spec, spec, spec, spec],
        out_shape=[jax.ShapeDtypeStruct((n, D), F32), jax.ShapeDtypeStruct((n, D), F32),
                   jax.ShapeDtypeStruct((n, D), _MXU_DTYPE), jax.ShapeDtypeStruct((n, D), _MXU_DTYPE)],
        compiler_params=pltpu.CompilerParams(
            dimension_semantics=("parallel",), vmem_limit_bytes=VMEM_LIMIT_BYTES),
        name="mem_kv",
    )(mem, lw['w_xk'], lw['w_xv'])


def _layer_weights(l, w_in, b_if, conv_w, gn_m, gn_r, w_out, w_xq, w_xk, w_xv, w_xo, ln_g, ln_b):
    wi = w_in[l]
    gates = wi[:, GATE_COL0:GATE_COL0 + N_GATES]
    b = b_if[l].astype(F32)
    return {
        'w_main': _mx(jnp.concatenate([wi[:, :GATE_COL0], wi[:, GATE_COL0 + N_GATES:]], axis=1)),
        'w_gc': _mx(jnp.pad(gates, ((0, 0), (0, LANES - N_GATES)))),
        'w_gr': _mx(gates.T),
        'b_col': jnp.pad(b, (0, LANES - N_GATES))[None, :],
        'b_row': b[:, None],
        'conv_w': conv_w[l].astype(F32),
        'gn': jnp.concatenate([gn_m[l], gn_r[l]]).astype(F32)[None, :],
        'w_out': _mx(w_out[l]), 'w_xq': _mx(w_xq[l]), 'w_xk': _mx(w_xk[l]), 'w_xv': _mx(w_xv[l]),
        'w_xo': _mx(w_xo[l]),
        'ln_g': ln_g[l].astype(F32), 'ln_b': ln_b[l].astype(F32),
    }


def kernel(x_prompt, x_sample, mem_prompt, cache_mem_k, cache_mem_v, state_conv, state_mlstm_C,
           state_mlstm_n, state_mlstm_m, state_ret_S, w_in, b_if, conv_w, gn_m, gn_r, w_out, w_xq,
           w_xk, w_xv, w_xo, w_router, w_e_gate, w_e_up, w_e_down, ln_g, ln_b):
    depth = w_in.shape[0]
    alpha = (2 * depth) ** 0.25
    Bp, Tp, D = x_prompt.shape
    Bs, Ts, _ = x_sample.shape
    pos_p = jnp.arange(Tp)
    pos_s = PAST_LEN + jnp.arange(Ts)
    w_router_t = w_router.astype(F32).T
    mem2d = mem_prompt.reshape(Bp * N_MEM, D)
    xp, xs = x_prompt, x_sample
    experts = tuple(w.reshape(depth * N_EXPERTS, *w.shape[2:]) for w in (w_e_gate, w_e_up, w_e_down))
    cnt0 = jnp.zeros((BUCKET_ROWS, LANES), F32)
    outs = [[] for _ in range(12)]
    for l in range(depth):
        lw = _layer_weights(l, w_in, b_if, conv_w, gn_m, gn_r, w_out, w_xq, w_xk, w_xv, w_xo, ln_g, ln_b)
        kp, vp, kph, vph = _mem_kv(mem2d, lw)
        x1, cb, c_n, n_n, m_n, s_n = _mixer(xp, lw, pos_p, None, alpha)
        x2_p, rb_p, cnt = _xattn(x1, kph.reshape(Bp, N_MEM, D), vph.reshape(Bp, N_MEM, D), lw,
                                        w_router_t, cnt0, alpha)
        for lst, val in zip(outs[:7], (kp.reshape(Bp, N_MEM, X_HEADS, X_DIM),
                                       vp.reshape(Bp, N_MEM, X_HEADS, X_DIM), cb, c_n, n_n, m_n, s_n)):
            lst.append(val)
        state = (state_conv[l], state_mlstm_C[l], state_mlstm_n[l],
                 jnp.broadcast_to(state_mlstm_m[l][:, :, None], (Bs, M_HEADS, LANES)), state_ret_S[l])
        x1, cb, c_n, n_n, m_n, s_n = _mixer(xs, lw, pos_s, state, alpha)
        x2_s, rb_s, cnt = _xattn(x1, _mx(cache_mem_k[l].reshape(Bs, N_MEM, D)),
                                        _mx(cache_mem_v[l].reshape(Bs, N_MEM, D)), lw, w_router_t, cnt, alpha)
        for lst, val in zip(outs[7:], (cb, c_n, n_n, m_n, s_n)):
            lst.append(val)
        xp2, xs2 = _moe(x2_p, rb_p, x2_s, rb_s, cnt, experts, l, lw, alpha)
        xp, xs = xp2.reshape(Bp, Tp, D), xs2.reshape(Bs, Ts, D)
    return (xp, xs) + tuple(jnp.stack(o) for o in outs)
```

```python
import functools

import jax
import jax.numpy as jnp
from jax import lax
from jax.experimental import pallas as pl
from jax.experimental.pallas import tpu as pltpu

F32 = jnp.float32
_MXU_DTYPE = jnp.bfloat16

D_MODEL = 1024
PAST_LEN = 4096
N_MEM = 256
M_WIDTH = D_MODEL // 2
M_HEADS = 4
M_DIM = M_WIDTH // M_HEADS
R_WIDTH = D_MODEL - M_WIDTH
R_HEADS = 4
R_DIM = R_WIDTH // R_HEADS
CONV_W = 4
ROPE_BASE = 10000.0
X_HEADS = 4
X_DIM = D_MODEL // X_HEADS
N_EXPERTS = 16
N_GROUPS = 4
EXP_PER_GROUP = N_EXPERTS // N_GROUPS
D_EXPERT = D_MODEL // 2
LN_EPS = 1e-5
GATE_COL0 = 2 * M_WIDTH + 2 * M_WIDTH
N_GATES = 2 * M_HEADS
MAIN_WIDTH = 4 * D_MODEL

LANES = 128
SUBLANES = 8
VMEM_LIMIT_BYTES = 56 * 1024 * 1024

PROMPT_BLOCK_ROWS = 512
PROMPT_CHUNK = 256
SAMPLE_SEQS_PER_BLOCK = 8

PAIRS = ((0, 1), (0, 2), (0, 3), (1, 2), (1, 3), (2, 3))
N_BUCKETS = N_GROUPS * len(PAIRS)
BUCKET_ROWS = 32
EXPERT_TILE = 256
ROW_WIDTH = D_MODEL + LANES
FEATURE_ROWS = D_MODEL // LANES
DMA_UNROLL = 8
COMBINE_ISSUE_STAGE = 64
COMBINE_LN_ROWS = 64
HEAD_GROUPS = ((('m', 0), ('m', 1)), (('r', 0), ('r', 1)), (('m', 2), ('m', 3)), (('r', 2), ('r', 3)))
PROJ_PIECE = 512
PROJ_EVERY = 3
SAMPLE_SEQS_INTERLEAVED = 2
XATTN_SEQ_GROUP = 4
ZERO_ROWS = EXPERT_TILE // 2


def _mx(a):
    return a.astype(_MXU_DTYPE)


def _dot(a, b):
    return jnp.dot(_mx(a), _mx(b), preferred_element_type=F32)


def _dot_nt(a, b):
    return lax.dot_general(_mx(a), _mx(b), (((1,), (1,)), ((), ())), preferred_element_type=F32)


def _dot_tn(a, b):
    return lax.dot_general(_mx(a), _mx(b), (((0,), (0,)), ((), ())), preferred_element_type=F32)


def _dot_exact(a, b):
    return jnp.dot(a, b, preferred_element_type=F32, precision=lax.Precision.HIGHEST)


def _layer_norm(y, g, b):
    mu = jnp.mean(y, axis=-1, keepdims=True)
    yc = y - mu
    var = jnp.mean(yc * yc, axis=-1, keepdims=True)
    return yc * lax.rsqrt(var + LN_EPS) * g + b


def _head_norm(h):
    mu = jnp.mean(h, axis=-1, keepdims=True)
    hc = h - mu
    var = jnp.mean(hc * hc, axis=-1, keepdims=True)
    return hc * lax.rsqrt(var + LN_EPS)


def _sigmoid(x):
    return 1.0 / (1.0 + jnp.exp(-x))


def _log_sigmoid(x):
    return jnp.minimum(x, 0.0) - jnp.log1p(jnp.exp(-jnp.abs(x)))


def _run_interleaved(gens):
    live = list(gens)
    while live:
        for g in list(live):
            try:
                next(g)
            except StopIteration:
                live.remove(g)


def _spread(gen, n, every):
    for i in range(n * every):
        if i % every == 0:
            try:
                next(gen)
            except StopIteration:
                return
        yield


def _const_spec(shape):
    n = len(shape)
    return pl.BlockSpec(shape, lambda *_: (0,) * n)


def _mixer_kernel(*refs, chunk, n_chunks, carried, alpha):
    (x_ref, wmain_ref, wgc_ref, wgr_ref, bcol_ref, brow_ref, convw_ref, gn_ref, wout_ref,
     lng_ref, lnb_ref, cos_ref, sin_ref, dmask_ref, cd_ref, wkr_ref, gl_ref) = refs[:17]
    if carried:
        state_in = None
        xn_ref = refs[17]
        rest = refs[18:]
    else:
        state_in = refs[17:22]
        rest = refs[22:]
    x1_ref, conv_out, c_out, n_out, m_out, s_out = rest[:6]
    proj_ref, gcol_ref, hbuf_ref, ub_ref, xh_ref = rest[6:]
    if carried:
        conv_in = None
        c_src, n_src, m_src, s_src = c_out, n_out, m_out, s_out
    else:
        conv_in, c_src, n_src, m_src, s_src = state_in
    L = chunk

    xb = x_ref[...]
    xh = _mx(xb)
    xh_ref[...] = xh

    def project(xh_rows, buf, rows):
        for j in range(MAIN_WIDTH // PROJ_PIECE):
            cols = slice(j * PROJ_PIECE, (j + 1) * PROJ_PIECE)
            proj_ref[buf, rows, cols] = jnp.dot(xh_rows, wmain_ref[:, cols], preferred_element_type=F32)
            yield
        gcol_ref[buf, rows, :] = jnp.dot(xh_rows, wgc_ref[...], preferred_element_type=F32) + bcol_ref[...]
        yield

    if carried:
        cur = pl.program_id(1) % 2
        nxt = 1 - cur

        @pl.when(pl.program_id(1) == 0)
        def _():
            for _ in project(xh, 0, slice(None)):
                pass
            ub_ref[0:SUBLANES, :] = jnp.zeros((SUBLANES, D_MODEL), F32)
            c_out[...] = jnp.zeros(c_out.shape, F32)
            n_out[...] = jnp.zeros(n_out.shape, F32)
            m_out[...] = jnp.zeros(m_out.shape, F32)
            s_out[...] = jnp.zeros(s_out.shape, F32)
    else:
        cur = 0
        for _ in project(xh, 0, slice(None)):
            pass

    row_i = lax.broadcasted_iota(jnp.int32, (L, L), 0)
    col_i = lax.broadcasted_iota(jnp.int32, (L, L), 1)
    causal = row_i >= col_i
    tri = causal.astype(F32)
    tri_t = (row_i <= col_i).astype(F32)
    convw = convw_ref[...]
    gn = gn_ref[...]

    def chunk_heads(c):
        r0 = pl.multiple_of(c * L, L)
        rows = pl.ds(r0, L)
        si = 0 if carried else c
        tab_rows = rows if carried else pl.ds(0, L)

        u = proj_ref[cur, rows, 0:2 * M_WIDTH]
        if not carried:
            ub_ref[SUBLANES - 3:SUBLANES, :] = conv_in[si]
        ub_ref[SUBLANES:SUBLANES + L, :] = u
        acc = u * convw[CONV_W - 1:CONV_W, :]
        for j in range(CONV_W - 1):
            acc = acc + ub_ref[SUBLANES - 3 + j:SUBLANES - 3 + j + L, :] * convw[j:j + 1, :]
        last_rows = ub_ref[SUBLANES + L - 3:SUBLANES + L, :]
        conv_out[si] = last_rows
        if carried:
            ub_ref[SUBLANES - 3:SUBLANES, :] = last_rows
        qk = acc * _sigmoid(acc)

        gc = gcol_ref[cur, rows, :]
        gr = _dot_nt(wgr_ref[...], xh_ref[rows, :]) + brow_ref[...]
        b_cols = _dot_exact(tri, _log_sigmoid(gc))
        b_rows = _dot_exact(_log_sigmoid(gr), tri_t)

        cos = cos_ref[tab_rows, :]
        sin = sin_ref[tab_rows, :]

        def mlstm_head(h):
            lo, hi = h * M_DIM, (h + 1) * M_DIM
            q = qk[:, lo:hi] * (M_DIM ** -0.5)
            k = qk[:, M_WIDTH + lo:M_WIDTH + hi]
            v = proj_ref[cur, rows, 2 * M_WIDTH + lo:2 * M_WIDTH + hi]
            c0 = c_src[si, h]
            n0 = n_src[si, h:h + 1, :]
            m0 = m_src[si, h:h + 1, 0:1]
            b_c = b_cols[:, M_HEADS + h:M_HEADS + h + 1]
            b_r = b_rows[M_HEADS + h:M_HEADS + h + 1, :]
            ig_c = gc[:, h:h + 1]
            ig_r = gr[h:h + 1, :]
            a_c = b_c + m0
            s_raw = _dot_nt(q, k)
            yield
            qc = _dot(q, c0)
            yield
            dmat = jnp.where(causal, b_c - b_r + ig_r, -jnp.inf)
            yield
            m_c = jnp.maximum(a_c, jnp.max(dmat, axis=-1, keepdims=True))
            yield
            w_intra = jnp.exp(dmat - m_c)
            yield
            w_inter = jnp.exp(a_c - m_c)
            s = s_raw * w_intra
            yield
            sv = _dot(s, v)
            yield
            m_last = m_c[L - 1:L, :]
            wk_c = jnp.exp(b_c[L - 1:L, :] - b_c + ig_c - m_last)
            decay = jnp.exp(a_c[L - 1:L, :] - m_last)
            kw = wk_c * k
            upd = _dot_tn(kw, v)
            yield
            num = sv + w_inter * qc
            den = jnp.sum(s, axis=-1, keepdims=True) + w_inter * jnp.sum(q * n0, axis=-1, keepdims=True)
            yield
            hm = num / jnp.maximum(jnp.abs(den), jnp.exp(-m_c))
            yield
            c_out[si, h] = decay * c0 + upd
            n_out[si, h:h + 1, :] = decay * n0 + jnp.sum(kw, axis=0, keepdims=True)
            m_out[si, h:h + 1, :] = jnp.broadcast_to(m_last, (1, LANES))
            yield
            og = proj_ref[cur, rows, 3 * M_WIDTH + lo:3 * M_WIDTH + hi]
            hm = _head_norm(hm) * gn[:, lo:hi] * _sigmoid(og)
            hbuf_ref[rows, lo:hi] = hm.astype(hbuf_ref.dtype)
            yield

        def retention_head(h):
            lo, hi = h * R_DIM, (h + 1) * R_DIM
            base = 4 * M_WIDTH
            rq = proj_ref[cur, rows, base + lo:base + hi]
            rk = proj_ref[cur, rows, base + R_WIDTH + lo:base + R_WIDTH + hi]
            v = proj_ref[cur, rows, base + 2 * R_WIDTH + lo:base + 2 * R_WIDTH + hi]
            q = rq * cos + pltpu.roll(rq, R_DIM // 2, 1) * sin
            yield
            k = (rk * cos + pltpu.roll(rk, R_DIM // 2, 1) * sin) * (R_DIM ** -0.5)
            s0 = s_src[si, h]
            yield
            s_raw = _dot_nt(q, k)
            yield
            qs = _dot(q, s0)
            yield
            upd = _dot_tn(wkr_ref[h] * k, v)
            yield
            s = s_raw * dmask_ref[h]
            yield
            sv = _dot(s, v)
            yield
            hr = sv + qs * cd_ref[h]
            s_out[si, h] = gl_ref[h] * s0 + upd
            yield
            rg = proj_ref[cur, rows, base + 3 * R_WIDTH + lo:base + 3 * R_WIDTH + hi]
            hr = _head_norm(hr) * gn[:, M_WIDTH + lo:M_WIDTH + hi] * (rg * _sigmoid(rg))
            hbuf_ref[rows, M_WIDTH + lo:M_WIDTH + hi] = hr.astype(hbuf_ref.dtype)
            yield

        return [[(mlstm_head if kind == 'm' else retention_head)(h) for kind, h in group]
                for group in HEAD_GROUPS]

    seq_par = 1 if carried else SAMPLE_SEQS_INTERLEAVED

    def chunk_body(i, carry):
        per_chunk = [chunk_heads(i * seq_par + j) for j in range(seq_par)]
        next_proj = None
        if carried:
            rows_n = pl.ds(pl.multiple_of(i * L, L), L)
            next_proj = project(_mx(xn_ref[rows_n, :]), nxt, rows_n)
        n_groups = len(per_chunk[0])
        for gi, groups in enumerate(zip(*per_chunk)):
            gens = [g for group in groups for g in group]
            if next_proj is not None:
                n_stage = -(-(MAIN_WIDTH // PROJ_PIECE + 1) // n_groups)
                gens.append(_spread(next_proj, n_stage if gi < n_groups - 1 else 2 * n_stage, PROJ_EVERY))
            _run_interleaved(gens)
        return carry

    lax.fori_loop(0, n_chunks // seq_par, chunk_body, 0)

    mix = jnp.dot(hbuf_ref[...], wout_ref[...], preferred_element_type=F32)
    x1_ref[...] = _layer_norm(alpha * xb + mix, lng_ref[...], lnb_ref[...])


def _retention_tables(L):
    log_g = jnp.log(1.0 - 2.0 ** (-5.0 - jnp.arange(R_HEADS, dtype=F32)))
    idx = jnp.arange(L, dtype=F32)
    rel = idx[:, None] - idx[None, :]
    dmask = jnp.where(rel >= 0, jnp.exp(log_g[:, None, None] * jnp.maximum(rel, 0.0)), 0.0)
    cd = jnp.exp(log_g[:, None] * (idx + 1.0))
    wk = jnp.exp(log_g[:, None] * (L - 1.0 - idx))
    gl = jnp.exp(log_g * L)
    bc = lambda t: jnp.broadcast_to(t[:, :, None], (R_HEADS, L, LANES))
    return dmask, bc(cd), bc(wk), jnp.broadcast_to(gl[:, None, None], (R_HEADS, 1, LANES))


def _rotary_tables(pos):
    half = R_DIM // 2
    inv = ROPE_BASE ** (-jnp.arange(half, dtype=F32) / half)
    ang = pos.astype(F32)[:, None] * inv[None, :]
    cos, sin = jnp.cos(ang), jnp.sin(ang)
    return jnp.concatenate([cos, cos], axis=-1), jnp.concatenate([-sin, sin], axis=-1)


def _mixer(x, lw, pos, state, alpha):
    S, T, D = x.shape
    carried = state is None
    if carried:
        block_rows = min(PROMPT_BLOCK_ROWS, T)
        L = min(PROMPT_CHUNK, block_rows)
        seqs = 1
        n_t = T // block_rows
        grid = (S, n_t)
    else:
        seqs = min(SAMPLE_SEQS_PER_BLOCK, S)
        L = T
        block_rows = seqs * T
        n_t = 1
        grid = (S // seqs, 1)
    n_chunks = block_rows // L
    xr = x.reshape(S * T, D)
    cos, sin = _rotary_tables(pos)
    dmask, cd, wkr, gl = _retention_tables(L)

    row_spec = pl.BlockSpec((block_rows, D), lambda b, t: (b * n_t + t, 0))
    tab_spec = pl.BlockSpec((block_rows if carried else T, LANES), lambda b, t: (t, 0))
    in_specs = [
        row_spec,
        _const_spec((D, MAIN_WIDTH)), _const_spec((D, LANES)), _const_spec((N_GATES, D)),
        _const_spec((1, LANES)), _const_spec((N_GATES, 1)), _const_spec((CONV_W, 2 * M_WIDTH)),
        _const_spec((1, D)), _const_spec((D, D)), _const_spec((1, D)), _const_spec((1, D)),
        tab_spec, tab_spec,
        _const_spec((R_HEADS, L, L)), _const_spec((R_HEADS, L, LANES)), _const_spec((R_HEADS, L, LANES)),
        _const_spec((R_HEADS, 1, LANES)),
    ]
    args = [xr, lw['w_main'], lw['w_gc'], lw['w_gr'], lw['b_col'], lw['b_row'], lw['conv_w'],
            lw['gn'], lw['w_out'], lw['ln_g'][0:1], lw['ln_b'][0:1], cos, sin, dmask, cd, wkr, gl]
    state_specs = [
        pl.BlockSpec((seqs, CONV_W - 1, 2 * M_WIDTH), lambda b, t: (b, 0, 0)),
        pl.BlockSpec((seqs, M_HEADS, M_DIM, M_DIM), lambda b, t: (b, 0, 0, 0)),
        pl.BlockSpec((seqs, M_HEADS, M_DIM), lambda b, t: (b, 0, 0)),
        pl.BlockSpec((seqs, M_HEADS, LANES), lambda b, t: (b, 0, 0)),
        pl.BlockSpec((seqs, R_HEADS, R_DIM, R_DIM), lambda b, t: (b, 0, 0, 0)),
    ]
    if carried:
        in_specs.append(pl.BlockSpec((block_rows, D), lambda b, t: (b * n_t + jnp.minimum(t + 1, n_t - 1), 0)))
        args.append(xr)
    else:
        in_specs += state_specs
        args += list(state)
    out_shape = [
        jax.ShapeDtypeStruct((S * T, D), F32),
        jax.ShapeDtypeStruct((S, CONV_W - 1, 2 * M_WIDTH), F32),
        jax.ShapeDtypeStruct((S, M_HEADS, M_DIM, M_DIM), F32),
        jax.ShapeDtypeStruct((S, M_HEADS, M_DIM), F32),
        jax.ShapeDtypeStruct((S, M_HEADS, LANES), F32),
        jax.ShapeDtypeStruct((S, R_HEADS, R_DIM, R_DIM), F32),
    ]
    out_specs = [row_spec] + state_specs
    scratch = [
        pltpu.VMEM((2 if carried else 1, block_rows, MAIN_WIDTH), F32),
        pltpu.VMEM((2 if carried else 1, block_rows, LANES), F32),
        pltpu.VMEM((block_rows, D), _MXU_DTYPE),
        pltpu.VMEM((SUBLANES + L, 2 * M_WIDTH), F32),
        pltpu.VMEM((block_rows, D), _MXU_DTYPE),
    ]
    outs = pl.pallas_call(
        functools.partial(_mixer_kernel, chunk=L, n_chunks=n_chunks, carried=carried, alpha=alpha),
        grid=grid, in_specs=in_specs, out_specs=out_specs, out_shape=out_shape,
        scratch_shapes=scratch,
        compiler_params=pltpu.CompilerParams(
            dimension_semantics=("parallel", "arbitrary"), vmem_limit_bytes=VMEM_LIMIT_BYTES),
        name="mixer_prompt" if carried else "mixer_sample",
    )(*args)
    x1, conv_new, c_new, n_new, m_new, s_new = outs
    return x1.reshape(S, T, D), conv_new, c_new, n_new, m_new[:, :, 0], s_new


def _route_rows(logits_t):
    mx = jnp.max(logits_t, axis=0, keepdims=True)
    ex = jnp.exp(logits_t - mx)
    probs = ex / jnp.sum(ex, axis=0, keepdims=True)
    p = [probs[e:e + 1, :] for e in range(N_EXPERTS)]

    def first_max(vals):
        m = vals[0]
        for t in vals[1:]:
            m = jnp.maximum(m, t)
        taken = None
        sel = []
        for t in vals:
            is_max = t == m
            if taken is None:
                pick, taken = is_max, is_max
            else:
                pick = jnp.logical_and(is_max, jnp.logical_not(taken))
                taken = jnp.logical_or(taken, is_max)
            sel.append(pick)
        return m, sel

    scores, picked, denom = [], [], []
    for g in range(N_GROUPS):
        vals = p[g * EXP_PER_GROUP:(g + 1) * EXP_PER_GROUP]
        m1, sel1 = first_max(vals)
        rest = [jnp.where(s1, -1.0, t) for s1, t in zip(sel1, vals)]
        m2, sel2 = first_max(rest)
        scores.append(m1 + m2)
        denom.append(m1 + m2)
        picked.append([jnp.logical_or(a, b) for a, b in zip(sel1, sel2)])
    _, gsel = first_max(scores)
    bucket = jnp.zeros(p[0].shape, jnp.int32)
    ca = jnp.zeros(p[0].shape, F32)
    cb = jnp.zeros(p[0].shape, F32)
    for g in range(N_GROUPS):
        w = [p[g * EXP_PER_GROUP + j] / denom[g] for j in range(EXP_PER_GROUP)]
        for pi, (a, b) in enumerate(PAIRS):
            hit = jnp.logical_and(gsel[g], jnp.logical_and(picked[g][a], picked[g][b]))
            bucket = jnp.where(hit, g * len(PAIRS) + pi, bucket)
            ca = jnp.where(hit, w[a], ca)
            cb = jnp.where(hit, w[b], cb)
    return bucket, ca, cb


def _xattn_kernel(x_ref, k_ref, v_ref, wq_ref, wo_ref, lng_ref, lnb_ref, wr_ref, upper_ref, cnt_in_ref,
                  x2w_ref, rb_ref, cnt_out_ref, q_ref, o_ref, run_ref, *, seqs, seq_rows, alpha):
    @pl.when(jnp.logical_and(pl.program_id(0) == 0, pl.program_id(1) == 0))
    def _():
        run_ref[...] = cnt_in_ref[...]

    xb = x_ref[...]
    q_ref[...] = _dot(xb, wq_ref[...]).astype(q_ref.dtype)
    scale = X_DIM ** -0.5
    def attend(sq, h):
        r0 = sq * seq_rows
        lo, hi = h * X_DIM, (h + 1) * X_DIM
        s = _dot_nt(q_ref[r0:r0 + seq_rows, lo:hi], k_ref[sq, :, lo:hi]) * scale
        yield
        e = jnp.exp(s - jnp.max(s, axis=-1, keepdims=True))
        yield
        p = e / jnp.sum(e, axis=-1, keepdims=True)
        yield
        o_ref[r0:r0 + seq_rows, lo:hi] = _dot(p, v_ref[sq, :, lo:hi]).astype(o_ref.dtype)

    for sq0 in range(0, seqs, XATTN_SEQ_GROUP):
        _run_interleaved([attend(sq, h) for sq in range(sq0, min(sq0 + XATTN_SEQ_GROUP, seqs))
                          for h in range(X_HEADS)])
    att = jnp.dot(o_ref[...], wo_ref[...], preferred_element_type=F32)
    x2 = _layer_norm(alpha * xb + att, lng_ref[...], lnb_ref[...])
    x2w_ref[:, 0:D_MODEL] = x2
    n = xb.shape[0]

    logits_t = lax.dot_general(wr_ref[...], x2, (((1,), (1,)), ((), ())),
                               preferred_element_type=F32, precision=lax.Precision.HIGHEST)
    bucket, ca, cb = _route_rows(logits_t)

    onehot = (lax.broadcasted_iota(jnp.int32, (BUCKET_ROWS, n), 0) == bucket).astype(F32)
    earlier = jnp.dot(_mx(onehot), upper_ref[...], preferred_element_type=F32)
    run = run_ref[...]
    rank = jnp.sum(onehot * (earlier + run[:, 0:1]), axis=0, keepdims=True)
    run_ref[...] = run + jnp.sum(onehot, axis=1, keepdims=True)
    cnt_out_ref[...] = run_ref[...]
    rb_ref[...] = jnp.concatenate(
        [bucket, rank.astype(jnp.int32), jnp.zeros((SUBLANES - 2, n), jnp.int32)], axis=0)

    cw_t = jnp.concatenate([ca, cb, jnp.zeros((LANES - 2, n), F32)], axis=0)
    x2w_ref[:, D_MODEL:ROW_WIDTH] = cw_t.T


def _xattn(x, mem_k, mem_v, lw, w_router_t, cnt_in, alpha):
    S, T, D = x.shape
    if T >= PROMPT_BLOCK_ROWS:
        seqs, seq_rows = 1, PROMPT_BLOCK_ROWS
    else:
        seqs, seq_rows = min(SAMPLE_SEQS_PER_BLOCK, S), T
    n_t = T // seq_rows
    block_rows = seqs * seq_rows
    grid = (S // seqs, n_t)
    row_map = lambda b, t: (b * n_t + t, 0)
    kv_spec = pl.BlockSpec((seqs, N_MEM, D), lambda b, t: (b, 0, 0))
    idx = jnp.arange(block_rows)
    upper = _mx(idx[:, None] < idx[None, :])
    return pl.pallas_call(
        functools.partial(_xattn_kernel, seqs=seqs, seq_rows=seq_rows, alpha=alpha),
        grid=grid,
        in_specs=[pl.BlockSpec((block_rows, D), row_map), kv_spec, kv_spec,
                  _const_spec((D, D)), _const_spec((D, D)), _const_spec((1, D)), _const_spec((1, D)),
                  _const_spec((N_EXPERTS, D)), _const_spec((block_rows, block_rows)),
                  _const_spec((BUCKET_ROWS, LANES))],
        out_specs=[pl.BlockSpec((block_rows, ROW_WIDTH), row_map),
                   pl.BlockSpec((SUBLANES, block_rows), lambda b, t: (0, b * n_t + t)),
                   _const_spec((BUCKET_ROWS, LANES))],
        out_shape=[jax.ShapeDtypeStruct((S * T, ROW_WIDTH), F32),
                   jax.ShapeDtypeStruct((SUBLANES, S * T), jnp.int32),
                   jax.ShapeDtypeStruct((BUCKET_ROWS, LANES), F32)],
        scratch_shapes=[pltpu.VMEM((block_rows, D), _MXU_DTYPE), pltpu.VMEM((block_rows, D), _MXU_DTYPE),
                        pltpu.VMEM((BUCKET_ROWS, LANES), F32)],
        compiler_params=pltpu.CompilerParams(
            dimension_semantics=("arbitrary", "arbitrary"), vmem_limit_bytes=VMEM_LIMIT_BYTES),
        name="xattn",
    )(x.reshape(S * T, D), mem_k, mem_v, lw['w_xq'], lw['w_xo'], lw['ln_g'][1:2], lw['ln_b'][1:2],
      w_router_t, upper, cnt_in)


def _route_tables(cnt, n_tiles):
    cnt = cnt[:N_BUCKETS, 0].astype(jnp.int32)
    padded = ((cnt + EXPERT_TILE - 1) // EXPERT_TILE) * EXPERT_TILE
    ends = jnp.cumsum(padded)
    base = ends - padded
    tile_start = jnp.arange(n_tiles, dtype=jnp.int32) * EXPERT_TILE
    valid = tile_start < ends[-1]
    last_start = jnp.maximum(ends[-1] - EXPERT_TILE, 0)
    start = jnp.where(valid, tile_start, last_start)
    tb = jnp.sum((ends[None, :] <= start[:, None]).astype(jnp.int32), axis=1)
    tb = jnp.minimum(tb, N_BUCKETS - 1)
    group, pair = tb // len(PAIRS), tb % len(PAIRS)
    pair_a = jnp.asarray([a for a, _ in PAIRS], jnp.int32)
    pair_b = jnp.asarray([b for _, b in PAIRS], jnp.int32)
    ea = group * EXP_PER_GROUP + pair_a[pair]
    eb = group * EXP_PER_GROUP + pair_b[pair]
    used_tiles = (ends[-1:] // EXPERT_TILE).astype(jnp.int32)
    return base, base + cnt, padded - cnt, used_tiles, ea, eb, valid.astype(jnp.int32)


def _dispatch_kernel(pos_ref, pad_start_ref, pad_len_ref, used_tiles_ref, tok_p_ref, tok_s_ref,
                     xs_ref, zero_ref, sem, pad_sem, *, blocks_p):
    i = pl.program_id(0)
    rows = tok_p_ref.shape[0]
    n_tiles = xs_ref.shape[0] // EXPERT_TILE

    def unused_tile(t, wait):
        cp = pltpu.make_async_copy(zero_ref, xs_ref.at[pl.ds(t * EXPERT_TILE, EXPERT_TILE)], pad_sem)
        cp.wait() if wait else cp.start()

    def pad_copies(b, wait):
        off = pad_start_ref[b]
        left = pad_len_ref[b]
        head = jnp.minimum((SUBLANES - off % SUBLANES) % SUBLANES, left)
        for j in range(SUBLANES - 1):
            cp = pltpu.make_async_copy(zero_ref.at[pl.ds(0, 1)], xs_ref.at[pl.ds(off + j, 1)], pad_sem)

            @pl.when(j < head)
            def _():
                cp.wait() if wait else cp.start()
        off = off + head
        left = left - head
        size = ZERO_ROWS
        while size >= SUBLANES:
            dst = xs_ref.at[pl.ds(pl.multiple_of(off, SUBLANES), size)]
            cp = pltpu.make_async_copy(zero_ref.at[pl.ds(0, size)], dst, pad_sem)

            @pl.when(left >= size)
            def _():
                cp.wait() if wait else cp.start()
            take = left >= size
            off = jnp.where(take, off + size, off)
            left = jnp.where(take, left - size, left)
            size //= 2

    @pl.when(i == 0)
    def _():
        zero_ref[...] = jnp.zeros(zero_ref.shape, F32)
        for wait in (False, True):
            lax.fori_loop(used_tiles_ref[0], n_tiles, lambda t, c, w=wait: (unused_tile(t, w), c)[1], 0)
            lax.fori_loop(0, N_BUCKETS, lambda b, c, w=wait: (pad_copies(b, w), c)[1], 0)

    def scatter(tok_ref):
        def issue(k, carry):
            for j in range(DMA_UNROLL):
                r = k * DMA_UNROLL + j
                pltpu.make_async_copy(tok_ref.at[pl.ds(r, 1)], xs_ref.at[pl.ds(pos_ref[i * rows + r], 1)],
                                      sem).start()
            return carry
        lax.fori_loop(0, rows // DMA_UNROLL, issue, 0)
        pltpu.make_async_copy(tok_ref, xs_ref.at[pl.ds(0, rows)], sem).wait()

    @pl.when(i < blocks_p)
    def _():
        scatter(tok_p_ref)

    @pl.when(i >= blocks_p)
    def _():
        scatter(tok_s_ref)


def _dispatch(tok_p, tok_s, pos, pad_start, pad_len, used_tiles, n_rows):
    n_p, n_s = tok_p.shape[0], tok_s.shape[0]
    rows = min(PROMPT_BLOCK_ROWS, n_p, n_s)
    blocks_p, blocks_s = n_p // rows, n_s // rows
    block = (rows, ROW_WIDTH)
    return pl.pallas_call(
        functools.partial(_dispatch_kernel, blocks_p=blocks_p),
        grid_spec=pltpu.PrefetchScalarGridSpec(
            num_scalar_prefetch=4, grid=(blocks_p + blocks_s,),
            in_specs=[pl.BlockSpec(block, lambda i, *_: (jnp.minimum(i, blocks_p - 1), 0)),
                      pl.BlockSpec(block, lambda i, *_: (jnp.maximum(i - blocks_p, 0), 0))],
            out_specs=pl.BlockSpec(memory_space=pl.ANY),
            scratch_shapes=[pltpu.VMEM((EXPERT_TILE, ROW_WIDTH), F32),
                            pltpu.SemaphoreType.DMA, pltpu.SemaphoreType.DMA]),
        out_shape=jax.ShapeDtypeStruct((n_rows, ROW_WIDTH), F32),
        compiler_params=pltpu.CompilerParams(
            dimension_semantics=("arbitrary",), vmem_limit_bytes=VMEM_LIMIT_BYTES),
        name="dispatch",
    )(pos, pad_start, pad_len, used_tiles, tok_p, tok_s)


def _expert_kernel(ea_ref, eb_ref, valid_ref, xs_ref, wga_ref, wua_ref, wda_ref, wgb_ref, wub_ref, wdb_ref,
                   ys_ref, ga_ref, ua_ref, da_ref, gb_ref, ub_ref, db_ref):
    t = pl.program_id(0)

    @pl.when(valid_ref[t] > 0)
    def _():
        prev = jnp.maximum(t - 1, 0)

        @pl.when(jnp.logical_or(t == 0, ea_ref[t] != ea_ref[prev]))
        def _():
            ga_ref[...] = _mx(wga_ref[0])
            ua_ref[...] = _mx(wua_ref[0])
            da_ref[...] = _mx(wda_ref[0])

        @pl.when(jnp.logical_or(t == 0, eb_ref[t] != eb_ref[prev]))
        def _():
            gb_ref[...] = _mx(wgb_ref[0])
            ub_ref[...] = _mx(wub_ref[0])
            db_ref[...] = _mx(wdb_ref[0])

        x = _mx(xs_ref[:, 0:D_MODEL])
        cw = xs_ref[:, D_MODEL:ROW_WIDTH]

        gate_a = jnp.dot(x, ga_ref[...], preferred_element_type=F32)
        gate_b = jnp.dot(x, gb_ref[...], preferred_element_type=F32)
        up_a = jnp.dot(x, ua_ref[...], preferred_element_type=F32)
        up_b = jnp.dot(x, ub_ref[...], preferred_element_type=F32)
        h_a = _mx(gate_a * _sigmoid(gate_a) * up_a)
        h_b = _mx(gate_b * _sigmoid(gate_b) * up_b)
        y_a = jnp.dot(h_a, da_ref[...], preferred_element_type=F32)
        y_b = jnp.dot(h_b, db_ref[...], preferred_element_type=F32)
        y = cw[:, 0:1] * y_a + cw[:, 1:2] * y_b
        for s in range(FEATURE_ROWS):
            ys_ref[pl.ds(s, EXPERT_TILE, stride=FEATURE_ROWS), :] = y[:, s * LANES:(s + 1) * LANES]

    @pl.when(valid_ref[t] == 0)
    def _():
        ys_ref[...] = jnp.zeros(ys_ref.shape, F32)


def _experts(xs, ea, eb, valid, w_e_gate, w_e_up, w_e_down):
    n_tiles = ea.shape[0]
    D = D_MODEL
    wa = lambda t, ea, eb, v: (ea[t], 0, 0)
    wb = lambda t, ea, eb, v: (eb[t], 0, 0)
    gu = (1, D, D_EXPERT)
    dn = (1, D_EXPERT, D)
    return pl.pallas_call(
        _expert_kernel,
        grid_spec=pltpu.PrefetchScalarGridSpec(
            num_scalar_prefetch=3, grid=(n_tiles,),
            in_specs=[pl.BlockSpec((EXPERT_TILE, ROW_WIDTH), lambda t, ea, eb, v: (t * v[t], 0)),
                      pl.BlockSpec(gu, wa), pl.BlockSpec(gu, wa), pl.BlockSpec(dn, wa),
                      pl.BlockSpec(gu, wb), pl.BlockSpec(gu, wb), pl.BlockSpec(dn, wb)],
            out_specs=pl.BlockSpec((EXPERT_TILE * FEATURE_ROWS, LANES), lambda t, ea, eb, v: (t, 0)),
            scratch_shapes=[pltpu.VMEM((D, D_EXPERT), _MXU_DTYPE), pltpu.VMEM((D, D_EXPERT), _MXU_DTYPE),
                            pltpu.VMEM((D_EXPERT, D), _MXU_DTYPE), pltpu.VMEM((D, D_EXPERT), _MXU_DTYPE),
                            pltpu.VMEM((D, D_EXPERT), _MXU_DTYPE), pltpu.VMEM((D_EXPERT, D), _MXU_DTYPE)]),
        out_shape=jax.ShapeDtypeStruct((n_tiles * EXPERT_TILE * FEATURE_ROWS, LANES), F32),
        compiler_params=pltpu.CompilerParams(
            dimension_semantics=("arbitrary",), vmem_limit_bytes=VMEM_LIMIT_BYTES),
        name="experts",
    )(ea, eb, valid, xs, w_e_gate, w_e_up, w_e_down, w_e_gate, w_e_up, w_e_down)


def _combine_kernel(pos_ref, x_ref, ys_ref, lng_ref, lnb_ref, out_ref, buf0_ref, buf1_ref, sem, *, alpha):
    i = pl.program_id(0)
    n_steps = pl.num_programs(0)
    half = x_ref.shape[0] // 2
    bufs = (buf0_ref, buf1_ref)

    def row_copy(block, slot, r):
        src = pl.multiple_of(pos_ref[block * half + r] * FEATURE_ROWS, FEATURE_ROWS)
        return pltpu.make_async_copy(ys_ref.at[pl.ds(src, FEATURE_ROWS)],
                                     bufs[slot].at[pl.ds(r * FEATURE_ROWS, FEATURE_ROWS)], sem.at[slot])

    def fetch_loop(block, slot):
        def issue(k, carry):
            for j in range(DMA_UNROLL):
                row_copy(block, slot, k * DMA_UNROLL + j).start()
            return carry
        lax.fori_loop(0, half // DMA_UNROLL, issue, 0)

    def fetch_stages(block, slot):
        for r in range(half):
            row_copy(block, slot, r).start()
            if r % COMBINE_ISSUE_STAGE == COMBINE_ISSUE_STAGE - 1:
                yield

    def wait_all(slot):
        pltpu.make_async_copy(ys_ref.at[pl.ds(0, half * FEATURE_ROWS)], bufs[slot], sem.at[slot]).wait()

    def finish_stages(slot):
        buf = bufs[slot]
        for r0 in range(0, half, COMBINE_LN_ROWS):
            y = jnp.concatenate([buf[pl.ds(r0 * FEATURE_ROWS + s, COMBINE_LN_ROWS, stride=FEATURE_ROWS), :]
                                 for s in range(FEATURE_ROWS)], axis=-1)
            rows = pl.ds(slot * half + r0, COMBINE_LN_ROWS)
            out_ref[rows, :] = _layer_norm(alpha * x_ref[rows, :] + y, lng_ref[...], lnb_ref[...])
            yield

    @pl.when(i == 0)
    def _():
        fetch_loop(0, 0)

    wait_all(0)
    _run_interleaved([fetch_stages(2 * i + 1, 1), finish_stages(0)])
    wait_all(1)
    following = jnp.where(i + 1 < n_steps, 2 * i + 2, 0)
    _run_interleaved([fetch_stages(following, 0), finish_stages(1)])

    @pl.when(i + 1 == n_steps)
    def _():
        wait_all(0)


def _combine(x2w, ys, pos, lw, alpha):
    n, D = x2w.shape[0], D_MODEL
    rows = min(2 * PROMPT_BLOCK_ROWS, n)
    row_map = lambda i, pos: (i, 0)
    half_buf = pltpu.VMEM((rows // 2 * FEATURE_ROWS, LANES), F32)
    return pl.pallas_call(
        functools.partial(_combine_kernel, alpha=alpha),
        grid_spec=pltpu.PrefetchScalarGridSpec(
            num_scalar_prefetch=1, grid=(n // rows,),
            in_specs=[pl.BlockSpec((rows, D), row_map), pl.BlockSpec(memory_space=pl.ANY),
                      pl.BlockSpec((1, D), lambda i, pos: (0, 0)), pl.BlockSpec((1, D), lambda i, pos: (0, 0))],
            out_specs=pl.BlockSpec((rows, D), row_map),
            scratch_shapes=[half_buf, half_buf, pltpu.SemaphoreType.DMA((2,))]),
        out_shape=jax.ShapeDtypeStruct((n, D), F32),
        compiler_params=pltpu.CompilerParams(
            dimension_semantics=("arbitrary",), vmem_limit_bytes=VMEM_LIMIT_BYTES),
        name="combine",
    )(pos, x2w, ys, lw['ln_g'][2:3], lw['ln_b'][2:3])


def _moe(x2_p, rb_p, x2_s, rb_s, cnt, experts, layer, lw, alpha):
    n_p, n_s = x2_p.shape[0], x2_s.shape[0]
    n_tiles = -(-(n_p + n_s + N_BUCKETS * (EXPERT_TILE - 1)) // EXPERT_TILE)
    base, pad_start, pad_len, used_tiles, ea, eb, valid = _route_tables(cnt, n_tiles)
    pos_p = base[rb_p[0]] + rb_p[1]
    pos_s = base[rb_s[0]] + rb_s[1]
    xs = _dispatch(x2_p, x2_s, jnp.concatenate([pos_p, pos_s]), pad_start, pad_len, used_tiles,
                   n_tiles * EXPERT_TILE)
    ys = _experts(xs, ea + layer * N_EXPERTS, eb + layer * N_EXPERTS, valid, *experts)
    return _combine(x2_p, ys, pos_p, lw, alpha), _combine(x2_s, ys, pos_s, lw, alpha)


def _memkv_kernel(mem_ref, wk_ref, wv_ref, k_ref, v_ref, kh_ref, vh_ref):
    mh = _mx(mem_ref[...])
    k = jnp.dot(mh, wk_ref[...], preferred_element_type=F32)
    v = jnp.dot(mh, wv_ref[...], preferred_element_type=F32)
    k_ref[...] = k
    v_ref[...] = v
    kh_ref[...] = k.astype(kh_ref.dtype)
    vh_ref[...] = v.astype(vh_ref.dtype)


def _mem_kv(mem, lw):
    n, D = mem.shape
    rows = min(PROMPT_BLOCK_ROWS, n)
    spec = pl.BlockSpec((rows, D), lambda i: (i, 0))
    return pl.pallas_call(
        _memkv_kernel,
        grid=(n // rows,),
        in_specs=[spec, _const_spec((D, D)), _const_spec((D, D))],
        out_specs=[spec, spec, spec, spec],
        out_shape=[jax.ShapeDtypeStruct((n, D), F32), jax.ShapeDtypeStruct((n, D), F32),
                   jax.ShapeDtypeStruct((n, D), _MXU_DTYPE), jax.ShapeDtypeStruct((n, D), _MXU_DTYPE)],
        compiler_params=pltpu.CompilerParams(
            dimension_semantics=("parallel",), vmem_limit_bytes=VMEM_LIMIT_BYTES),
        name="mem_kv",
    )(mem, lw['w_xk'], lw['w_xv'])


def _layer_weights(l, w_in, b_if, conv_w, gn_m, gn_r, w_out, w_xq, w_xk, w_xv, w_xo, ln_g, ln_b):
    wi = w_in[l]
    gates = wi[:, GATE_COL0:GATE_COL0 + N_GATES]
    b = b_if[l].astype(F32)
    return {
        'w_main': _mx(jnp.concatenate([wi[:, :GATE_COL0], wi[:, GATE_COL0 + N_GATES:]], axis=1)),
        'w_gc': _mx(jnp.pad(gates, ((0, 0), (0, LANES - N_GATES)))),
        'w_gr': _mx(gates.T),
        'b_col': jnp.pad(b, (0, LANES - N_GATES))[None, :],
        'b_row': b[:, None],
        'conv_w': conv_w[l].astype(F32),
        'gn': jnp.concatenate([gn_m[l], gn_r[l]]).astype(F32)[None, :],
        'w_out': _mx(w_out[l]), 'w_xq': _mx(w_xq[l]), 'w_xk': _mx(w_xk[l]), 'w_xv': _mx(w_xv[l]),
        'w_xo': _mx(w_xo[l]),
        'ln_g': ln_g[l].astype(F32), 'ln_b': ln_b[l].astype(F32),
    }


def kernel(x_prompt, x_sample, mem_prompt, cache_mem_k, cache_mem_v, state_conv, state_mlstm_C,
           state_mlstm_n, state_mlstm_m, state_ret_S, w_in, b_if, conv_w, gn_m, gn_r, w_out, w_xq,
           w_xk, w_xv, w_xo, w_router, w_e_gate, w_e_up, w_e_down, ln_g, ln_b):
    depth = w_in.shape[0]
    alpha = (2 * depth) ** 0.25
    Bp, Tp, D = x_prompt.shape
    Bs, Ts, _ = x_sample.shape
    pos_p = jnp.arange(Tp)
    pos_s = PAST_LEN + jnp.arange(Ts)
    w_router_t = w_router.astype(F32).T
    mem2d = mem_prompt.reshape(Bp * N_MEM, D)
    xp, xs = x_prompt, x_sample
    experts = tuple(w.reshape(depth * N_EXPERTS, *w.shape[2:]) for w in (w_e_gate, w_e_up, w_e_down))
    cnt0 = jnp.zeros((BUCKET_ROWS, LANES), F32)
    outs = [[] for _ in range(12)]
    for l in range(depth):
        lw = _layer_weights(l, w_in, b_if, conv_w, gn_m, gn_r, w_out, w_xq, w_xk, w_xv, w_xo, ln_g, ln_b)
        kp, vp, kph, vph = _mem_kv(mem2d, lw)
        x1, cb, c_n, n_n, m_n, s_n = _mixer(xp, lw, pos_p, None, alpha)
        x2_p, rb_p, cnt = _xattn(x1, kph.reshape(Bp, N_MEM, D), vph.reshape(Bp, N_MEM, D), lw,
                                        w_router_t, cnt0, alpha)
        for lst, val in zip(outs[:7], (kp.reshape(Bp, N_MEM, X_HEADS, X_DIM),
                                       vp.reshape(Bp, N_MEM, X_HEADS, X_DIM), cb, c_n, n_n, m_n, s_n)):
            lst.append(val)
        state = (state_conv[l], state_mlstm_C[l], state_mlstm_n[l],
                 jnp.broadcast_to(state_mlstm_m[l][:, :, None], (Bs, M_HEADS, LANES)), state_ret_S[l])
        x1, cb, c_n, n_n, m_n, s_n = _mixer(xs, lw, pos_s, state, alpha)
        x2_s, rb_s, cnt = _xattn(x1, _mx(cache_mem_k[l].reshape(Bs, N_MEM, D)),
                                        _mx(cache_mem_v[l].reshape(Bs, N_MEM, D)), lw, w_router_t, cnt, alpha)
        for lst, val in zip(outs[7:], (cb, c_n, n_n, m_n, s_n)):
            lst.append(val)
        xp2, xs2 = _moe(x2_p, rb_p, x2_s, rb_s, cnt, experts, l, lw, alpha)
        xp, xs = xp2.reshape(Bp, Tp, D), xs2.reshape(Bs, Ts, D)
    return (xp, xs) + tuple(jnp.stack(o) for o in outs)
```
